```python
import math
import jax, jax.numpy as jnp
from jax import lax
import numpy as np

D_MODEL = 1024
BATCH = 2
SEQ = 8192
DEPTH = 2

GRID_W = 64
CTX_LEN = 256
HEAD_DIM = 64
GQA_HEADS = 8
GQA_KV_HEADS = 2
GQA_GROUP = GQA_HEADS // GQA_KV_HEADS
NA_HEADS = 8
NA_KH = 8
NA_KW = 16
MLA_HEADS = 8
MLA_NOPE = 64
MLA_ROPE = 32
MLA_V = 64
MLA_KV_RANK = 256
N_BRANCH = 3
D_FF = 4 * D_MODEL
Q_BLOCK = 128
ROPE_THETA = 10000.0
EPS = 1e-6
IN_SIZES = (
    GQA_HEADS * HEAD_DIM,
    GQA_KV_HEADS * HEAD_DIM,
    GQA_KV_HEADS * HEAD_DIM,
    NA_HEADS * HEAD_DIM,
    NA_HEADS * HEAD_DIM,
    NA_HEADS * HEAD_DIM,
    MLA_HEADS * (MLA_NOPE + MLA_ROPE),
    MLA_KV_RANK,
    MLA_ROPE,
    N_BRANCH * D_MODEL,
)
W_IN_COLS = sum(IN_SIZES)

kernel_name = 'hybrid_gqa_natten_mla_dit_block'


def rmsnorm(x, g):
    xf = x.astype(jnp.float32)
    y = xf * lax.rsqrt(jnp.mean(xf * xf, axis=-1, keepdims=True) + EPS)
    return (y * g.astype(jnp.float32)).astype(x.dtype)


def modulate(h, shift, scale):
    return h * (1 + scale) + shift


def split_cols(p):
    pts = []
    acc = 0
    for sz in IN_SIZES[:-1]:
        acc += sz
        pts.append(acc)
    return jnp.split(p, pts, axis=-1)


def axial_angles(n_tokens, rot_dim):
    t = jnp.arange(n_tokens, dtype=jnp.int32)
    row = (t // GRID_W).astype(jnp.float32)
    col = (t % GRID_W).astype(jnp.float32)
    half = rot_dim // 2
    inv_freq = ROPE_THETA ** (-jnp.arange(0, half, 2, dtype=jnp.float32) / half)
    return row[:, None] * inv_freq, col[:, None] * inv_freq


def rope_1d(x, ang):
    cos = jnp.cos(ang)[:, None, :].astype(x.dtype)
    sin = jnp.sin(ang)[:, None, :].astype(x.dtype)
    x1, x2 = jnp.split(x, 2, axis=-1)
    return jnp.concatenate([x1 * cos - x2 * sin, x2 * cos + x1 * sin], axis=-1)


def rope_axial(x, angles):
    ang_row, ang_col = angles
    xr, xc = jnp.split(x, 2, axis=-1)
    return jnp.concatenate([rope_1d(xr, ang_row), rope_1d(xc, ang_col)], axis=-1)


def sdpa_blocks(q, k, v, scale):
    b, s = q.shape[:2]
    nb = s // Q_BLOCK
    qb = jnp.swapaxes(q.reshape((b, nb, Q_BLOCK) + q.shape[2:]), 0, 1)

    def one(qblk):
        sc = jnp.einsum('bqkgd,btkd->bkgqt', qblk, k).astype(jnp.float32) * scale
        p = jax.nn.softmax(sc, axis=-1).astype(v.dtype)
        return jnp.einsum('bkgqt,btke->bqkge', p, v)

    o = lax.map(one, qb)
    return jnp.swapaxes(o, 0, 1).reshape(b, s, q.shape[2] * q.shape[3] * v.shape[-1])


def neighbourhood_attention(q, k, v, k_ctx, v_ctx, rpb):
    b, s, h, d = q.shape
    rows = s // GRID_W
    kh = min(NA_KH, rows)
    kw = NA_KW
    scale = d ** -0.5
    qg = q.reshape(b, rows, GRID_W, h, d)
    kg = k.reshape(b, rows, GRID_W, h, d)
    vg = v.reshape(b, rows, GRID_W, h, d)
    cols = jnp.arange(GRID_W, dtype=jnp.int32)
    col_start = jnp.clip(cols - kw // 2, 0, GRID_W - kw)
    col_idx = col_start[:, None] + jnp.arange(kw, dtype=jnp.int32)[None, :]
    col_bias_idx = col_idx - cols[:, None] + (NA_KW - 1)
    n_win = kh * kw

    def one(args):
        r, q_row = args
        rs = jnp.clip(r - kh // 2, 0, rows - kh)
        k_rows = lax.dynamic_slice_in_dim(kg, rs, kh, axis=1)
        v_rows = lax.dynamic_slice_in_dim(vg, rs, kh, axis=1)
        k_win = k_rows[:, :, col_idx]
        v_win = v_rows[:, :, col_idx]
        row_bias_idx = rs + jnp.arange(kh, dtype=jnp.int32) - r + (NA_KH - 1)
        bias = rpb[:, row_bias_idx][:, :, col_bias_idx]
        bias = jnp.transpose(bias, (0, 2, 1, 3))[None].astype(jnp.float32)
        s_win = jnp.einsum('bqhd,biqjhd->bhqij', q_row, k_win).astype(jnp.float32) * scale + bias
        s_ctx = jnp.einsum('bqhd,bthd->bhqt', q_row, k_ctx).astype(jnp.float32) * scale
        sc = jnp.concatenate([s_win.reshape(b, h, GRID_W, n_win), s_ctx], axis=-1)
        p = jax.nn.softmax(sc, axis=-1).astype(v.dtype)
        p_win = p[..., :n_win].reshape(b, h, GRID_W, kh, kw)
        p_ctx = p[..., n_win:]
        return (jnp.einsum('bhqij,biqjhd->bqhd', p_win, v_win)
                + jnp.einsum('bhqt,bthd->bqhd', p_ctx, v_ctx))

    o = lax.map(one, (jnp.arange(rows, dtype=jnp.int32), jnp.swapaxes(qg, 0, 1)))
    return jnp.swapaxes(o, 0, 1).reshape(b, s, h * d)


def project(h, w_in, q_norm, k_norm, kv_norm, w_uk, w_uv, ang_a, ang_m):
    b, n, _ = h.shape
    (ga_q, ga_k, ga_v, na_q, na_k, na_v, ml_q, ml_ckv, ml_kr, gates) = split_cols(h @ w_in)
    ga_q = rmsnorm(ga_q.reshape(b, n, GQA_HEADS, HEAD_DIM), q_norm)
    ga_k = rmsnorm(ga_k.reshape(b, n, GQA_KV_HEADS, HEAD_DIM), k_norm)
    ga_v = ga_v.reshape(b, n, GQA_KV_HEADS, HEAD_DIM)
    na_q = na_q.reshape(b, n, NA_HEADS, HEAD_DIM)
    na_k = na_k.reshape(b, n, NA_HEADS, HEAD_DIM)
    na_v = na_v.reshape(b, n, NA_HEADS, HEAD_DIM)
    ml_q = ml_q.reshape(b, n, MLA_HEADS, MLA_NOPE + MLA_ROPE)
    q_nope, q_rope = ml_q[..., :MLA_NOPE], ml_q[..., MLA_NOPE:]
    c_kv = rmsnorm(ml_ckv, kv_norm)
    k_nope = (c_kv @ w_uk).reshape(b, n, MLA_HEADS, MLA_NOPE)
    ml_v = (c_kv @ w_uv).reshape(b, n, MLA_HEADS, MLA_V)
    k_rope = ml_kr[:, :, None, :]
    if ang_a is not None:
        ga_q = rope_axial(ga_q, ang_a)
        ga_k = rope_axial(ga_k, ang_a)
        q_rope = rope_axial(q_rope, ang_m)
        k_rope = rope_axial(k_rope, ang_m)
    ml_q = jnp.concatenate([q_nope, q_rope], axis=-1)
    ml_k = jnp.concatenate([k_nope, jnp.broadcast_to(k_rope, (b, n, MLA_HEADS, MLA_ROPE))], axis=-1)
    gates = jax.nn.sigmoid(gates.astype(jnp.float32)).astype(h.dtype).reshape(b, n, N_BRANCH, D_MODEL)
    return {'ga_q': ga_q, 'ga_k': ga_k, 'ga_v': ga_v, 'na_q': na_q, 'na_k': na_k, 'na_v': na_v,
            'ml_q': ml_q, 'ml_k': ml_k, 'ml_v': ml_v, 'gates': gates}


def merge(ya, yb, yc, gates, w_o_gqa, w_o_na, w_o_mla, w_out):
    y = (gates[..., 0, :] * (ya @ w_o_gqa)
         + gates[..., 1, :] * (yb @ w_o_na)
         + gates[..., 2, :] * (yc @ w_o_mla))
    return y @ w_out


def parallel_mixer(h_lat, h_ctx, w_in, q_norm, k_norm, rpb, kv_norm, w_uk, w_uv,
                   w_o_gqa, w_o_na, w_o_mla, w_out, ang_a, ang_m, with_ctx):
    b, s, _ = h_lat.shape
    pl = project(h_lat, w_in, q_norm, k_norm, kv_norm, w_uk, w_uv, ang_a, ang_m)
    pc = project(h_ctx, w_in, q_norm, k_norm, kv_norm, w_uk, w_uv, None, None)
    sc_a = HEAD_DIM ** -0.5
    sc_m = (MLA_NOPE + MLA_ROPE) ** -0.5
    k_all = jnp.concatenate([pl['ga_k'], pc['ga_k']], axis=1)
    v_all = jnp.concatenate([pl['ga_v'], pc['ga_v']], axis=1)
    ya = sdpa_blocks(pl['ga_q'].reshape(b, s, GQA_KV_HEADS, GQA_GROUP, HEAD_DIM), k_all, v_all, sc_a)
    yb = neighbourhood_attention(pl['na_q'], pl['na_k'], pl['na_v'], pc['na_k'], pc['na_v'], rpb)
    mk_all = jnp.concatenate([pl['ml_k'], pc['ml_k']], axis=1)
    mv_all = jnp.concatenate([pl['ml_v'], pc['ml_v']], axis=1)
    yc = sdpa_blocks(pl['ml_q'][:, :, :, None, :], mk_all, mv_all, sc_m)
    out_lat = merge(ya, yb, yc, pl['gates'], w_o_gqa, w_o_na, w_o_mla, w_out)
    if not with_ctx:
        return out_lat, None
    bc, cl, _ = h_ctx.shape
    ca = sdpa_blocks(pc['ga_q'].reshape(bc, cl, GQA_KV_HEADS, GQA_GROUP, HEAD_DIM), pc['ga_k'], pc['ga_v'], sc_a)
    cb = sdpa_blocks(pc['na_q'][:, :, :, None, :], pc['na_k'], pc['na_v'], sc_a)
    cc = sdpa_blocks(pc['ml_q'][:, :, :, None, :], pc['ml_k'], pc['ml_v'], sc_m)
    out_ctx = merge(ca, cb, cc, pc['gates'], w_o_gqa, w_o_na, w_o_mla, w_out)
    return out_lat, out_ctx


def sq_relu_mlp(h, w1, w2):
    return jnp.square(jax.nn.relu(h @ w1)) @ w2


def setup_inputs(seed: int = 0) -> dict:
    key = jax.random.key(seed)
    ks = jax.random.split(key, 24)
    d = D_MODEL

    def nrm(k, shape, scale):
        return jax.random.normal(k, shape, jnp.float32) * scale

    def gain(k, shape):
        return 1.0 + 0.01 * jax.random.normal(k, shape, jnp.float32)

    return {
        'x': nrm(ks[0], (BATCH, SEQ, d), 1.0),
        'c': nrm(ks[1], (BATCH, d), 1.0),
        'ctx': nrm(ks[2], (BATCH, CTX_LEN, d), 1.0),
        'c_ctx': nrm(ks[3], (d,), 1.0),
        'w_mod': nrm(ks[4], (DEPTH, d, 6 * d), 0.5 * d ** -0.5),
        'b_mod': nrm(ks[5], (DEPTH, 6 * d), 0.02),
        'norm1_g': gain(ks[6], (DEPTH, d)),
        'norm2_g': gain(ks[7], (DEPTH, d)),
        'w_in': nrm(ks[8], (DEPTH, d, W_IN_COLS), d ** -0.5),
        'gqa_q_norm': gain(ks[9], (DEPTH, HEAD_DIM)),
        'gqa_k_norm': gain(ks[10], (DEPTH, HEAD_DIM)),
        'na_rpb': nrm(ks[11], (DEPTH, NA_HEADS, 2 * NA_KH - 1, 2 * NA_KW - 1), 0.1),
        'mla_kv_norm': gain(ks[12], (DEPTH, MLA_KV_RANK)),
        'mla_w_uk': nrm(ks[13], (DEPTH, MLA_KV_RANK, MLA_HEADS * MLA_NOPE), MLA_KV_RANK ** -0.5),
        'mla_w_uv': nrm(ks[14], (DEPTH, MLA_KV_RANK, MLA_HEADS * MLA_V), MLA_KV_RANK ** -0.5),
        'w_o_gqa': nrm(ks[15], (DEPTH, GQA_HEADS * HEAD_DIM, d), (GQA_HEADS * HEAD_DIM) ** -0.5),
        'w_o_na': nrm(ks[16], (DEPTH, NA_HEADS * HEAD_DIM, d), (NA_HEADS * HEAD_DIM) ** -0.5),
        'w_o_mla': nrm(ks[17], (DEPTH, MLA_HEADS * MLA_V, d), (MLA_HEADS * MLA_V) ** -0.5),
        'w_out': nrm(ks[18], (DEPTH, d, d), d ** -0.5),
        'w_mlp1': nrm(ks[19], (DEPTH, d, D_FF), d ** -0.5),
        'w_mlp2': nrm(ks[20], (DEPTH, D_FF, d), D_FF ** -0.5),
        'final_norm_g': gain(ks[21], (d,)),
    }


def reference(x, c, ctx, c_ctx, w_mod, b_mod, norm1_g, norm2_g, w_in, gqa_q_norm, gqa_k_norm,
              na_rpb, mla_kv_norm, mla_w_uk, mla_w_uv, w_o_gqa, w_o_na, w_o_mla, w_out,
              w_mlp1, w_mlp2, final_norm_g):
    s = x.shape[1]
    ang_a = axial_angles(s, HEAD_DIM)
    ang_m = axial_angles(s, MLA_ROPE)
    cond_lat = jax.nn.silu(c)
    cond_ctx = jax.nn.silu(c_ctx)[None, :]
    for l in range(DEPTH):
        with_ctx = l < DEPTH - 1
        m_lat = (cond_lat @ w_mod[l] + b_mod[l])[:, None, :]
        m_ctx = (cond_ctx @ w_mod[l] + b_mod[l])[:, None, :]
        sh1, sc1, g1, sh2, sc2, g2 = jnp.split(m_lat, 6, axis=-1)
        csh1, csc1, cg1, csh2, csc2, cg2 = jnp.split(m_ctx, 6, axis=-1)
        h_lat = modulate(rmsnorm(x, norm1_g[l]), sh1, sc1)
        h_ctx = modulate(rmsnorm(ctx, norm1_g[l]), csh1, csc1)
        a_lat, a_ctx = parallel_mixer(h_lat, h_ctx, w_in[l], gqa_q_norm[l], gqa_k_norm[l], na_rpb[l],
                                      mla_kv_norm[l], mla_w_uk[l], mla_w_uv[l], w_o_gqa[l], w_o_na[l],
                                      w_o_mla[l], w_out[l], ang_a, ang_m, with_ctx)
        x = x + g1 * a_lat
        x = x + g2 * sq_relu_mlp(modulate(rmsnorm(x, norm2_g[l]), sh2, sc2), w_mlp1[l], w_mlp2[l])
        if with_ctx:
            ctx = ctx + cg1 * a_ctx
            ctx = ctx + cg2 * sq_relu_mlp(modulate(rmsnorm(ctx, norm2_g[l]), csh2, csc2), w_mlp1[l], w_mlp2[l])
    return rmsnorm(x, final_norm_g)
```

```python
import functools

import jax
import jax.numpy as jnp
from jax import lax
from jax.experimental import pallas as pl
from jax.experimental.pallas import tpu as pltpu

F32 = jnp.float32
BF16 = jnp.bfloat16

D_MODEL = 1024
GRID_W = 64
HEAD_DIM = 64
GQA_HEADS = 8
GQA_KV_HEADS = 2
NA_HEADS = 8
NA_KH = 8
NA_KW = 16
MLA_HEADS = 8
MLA_NOPE = 64
MLA_ROPE = 32
MLA_QK = MLA_NOPE + MLA_ROPE
MLA_KV_RANK = 256
D_FF = 4 * D_MODEL
ROPE_THETA = 10000.0
EPS = 1e-6

O_QA = 0
O_KA = O_QA + GQA_HEADS * HEAD_DIM
O_VA = O_KA + GQA_KV_HEADS * HEAD_DIM
O_QN = O_VA + GQA_KV_HEADS * HEAD_DIM
O_KN = O_QN + NA_HEADS * HEAD_DIM
O_VN = O_KN + NA_HEADS * HEAD_DIM
O_QM = O_VN + NA_HEADS * HEAD_DIM
O_CKV = O_QM + MLA_HEADS * MLA_QK
O_KR = O_CKV + MLA_KV_RANK
O_GATE = O_KR + MLA_ROPE

TM = 256
KPAD = 128
NA_QROWS = 4
NA_KBLK = 3
NEG = -1e30
VMEM_LIMIT = 56 * 1024 * 1024


def _cparams(n_axes):
    return pltpu.CompilerParams(dimension_semantics=("parallel",) * n_axes,
                                vmem_limit_bytes=VMEM_LIMIT)


def _const_spec(shape):
    nd = len(shape)
    return pl.BlockSpec(shape, lambda *_: (0,) * nd, pipeline_mode=pl.Buffered(1))


def _rms_mod(x, g, scale, shift):
    ms = jnp.mean(x * x, axis=-1, keepdims=True)
    return (x * lax.rsqrt(ms + EPS) * g) * (1.0 + scale) + shift


def _rms_fm(x, g):
    ms = jnp.mean(x * x, axis=0, keepdims=True)
    return x * lax.rsqrt(ms + EPS) * g


def _rope_fm(x, cos, sin, blk):
    sw = jnp.concatenate([x[blk:2 * blk], x[0:blk], x[3 * blk:4 * blk], x[2 * blk:3 * blk]], axis=0)
    return x * cos + sw * sin


def _mod_kernel(c_ref, w_ref, b_ref, o_ref):
    c = c_ref[...]
    c = (c * jax.nn.sigmoid(c)).astype(BF16)
    w = w_ref[0].astype(BF16)
    o_ref[0] = jnp.dot(c, w, preferred_element_type=F32) + b_ref[0]


def _modulation(cond, w_mod, b_mod):
    depth, d, n = w_mod.shape
    rows = cond.shape[0]
    tn = 1536
    return pl.pallas_call(
        _mod_kernel,
        grid=(depth, n // tn),
        in_specs=[pl.BlockSpec((rows, d), lambda l, j: (0, 0)),
                  pl.BlockSpec((1, d, tn), lambda l, j: (l, 0, j)),
                  pl.BlockSpec((1, 1, tn), lambda l, j: (l, 0, j))],
        out_specs=pl.BlockSpec((1, rows, tn), lambda l, j: (l, 0, j)),
        out_shape=jax.ShapeDtypeStruct((depth, rows, n), F32),
        compiler_params=_cparams(2),
        name="modulation",
    )(cond, w_mod, b_mod.reshape(depth, 1, n))


def _proj_kernel(*refs, use_rope):
    (x_ref, mod_ref, g_ref, w_ref, gq_ref, gk_ref, gkv_ref, wuk_ref, wuv_ref) = refs[:9]
    pos = 9
    if use_rope:
        cosa_ref, sina_ref, cosm_ref, sinm_ref = refs[pos:pos + 4]
        pos += 4
    (qa_o, ka_o, va_o, qn_o, kn_o, vn_o, qm_o, km_o, vm_o) = refs[pos:]

    d = D_MODEL
    x = x_ref[...]
    m = mod_ref[0]
    h = _rms_mod(x, g_ref[...], m[:, d:2 * d], m[:, 0:d]).astype(BF16)
    tm = x.shape[0]

    def seg(a, b):
        return lax.dot_general(w_ref[a:b, :], h, (((1,), (1,)), ((), ())),
                               preferred_element_type=F32)

    def rope_a(t):
        return _rope_fm(t, cosa_ref[...], sina_ref[...], HEAD_DIM // 4) if use_rope else t

    def rope_m(t):
        return _rope_fm(t, cosm_ref[...], sinm_ref[...], MLA_ROPE // 4) if use_rope else t

    zeros64 = jnp.zeros((HEAD_DIM, tm), BF16)

    p = seg(O_QA, O_KA)
    group = GQA_HEADS // GQA_KV_HEADS
    for hh in range(GQA_HEADS):
        q = rope_a(_rms_fm(p[hh * HEAD_DIM:(hh + 1) * HEAD_DIM], gq_ref[...])) * (HEAD_DIM ** -0.5)
        kvh = hh // group
        for j in range(KPAD // HEAD_DIM):
            qa_o[hh, j * HEAD_DIM:(j + 1) * HEAD_DIM, :] = q.astype(BF16) if j == kvh else zeros64

    p = seg(O_KA, O_VA)
    ks = [rope_a(_rms_fm(p[g * HEAD_DIM:(g + 1) * HEAD_DIM], gk_ref[...])) for g in range(GQA_KV_HEADS)]
    ka_o[...] = jnp.concatenate(ks, axis=0).T.astype(BF16)
    va_o[0] = seg(O_VA, O_QN).astype(BF16)

    p = seg(O_QN, O_KN)
    for hh in range(NA_HEADS):
        q = (p[hh * HEAD_DIM:(hh + 1) * HEAD_DIM] * (HEAD_DIM ** -0.5)).astype(BF16)
        for j in range(KPAD // HEAD_DIM):
            qn_o[hh, j * HEAD_DIM:(j + 1) * HEAD_DIM, :] = q if j == hh % 2 else zeros64
    kn_o[...] = seg(O_KN, O_VN).T.astype(BF16)
    vn_o[0] = seg(O_VN, O_QM).astype(BF16)

    p = seg(O_QM, O_CKV)
    zeros_pad = jnp.zeros((KPAD - MLA_QK, tm), BF16)
    for hh in range(MLA_HEADS):
        base = hh * MLA_QK
        qm_o[hh, 0:MLA_NOPE, :] = p[base:base + MLA_NOPE].astype(BF16)
        qm_o[hh, MLA_NOPE:MLA_QK, :] = rope_m(p[base + MLA_NOPE:base + MLA_QK]).astype(BF16)
        qm_o[hh, MLA_QK:KPAD, :] = zeros_pad

    ckv = _rms_fm(seg(O_CKV, O_KR), gkv_ref[...]).astype(BF16)
    k_nope = jnp.dot(wuk_ref[...], ckv, preferred_element_type=F32)
    vm_o[0] = jnp.dot(wuv_ref[...], ckv, preferred_element_type=F32).astype(BF16)
    k_rope = rope_m(seg(O_KR, O_GATE))
    zpad = jnp.zeros((KPAD - MLA_QK, tm), F32)
    blocks = []
    for hh in range(MLA_HEADS):
        blocks += [k_nope[hh * MLA_NOPE:(hh + 1) * MLA_NOPE], k_rope, zpad]
    km_o[...] = jnp.concatenate(blocks, axis=0).T.astype(BF16)


def _project(tok, mod, tiles_per_mod, g1, w_t, gq, gk, gkv, wuk_t, wuv_t, rope):
    t, d = tok.shape
    nt = t // TM
    use_rope = rope is not None
    in_specs = [
        pl.BlockSpec((TM, d), lambda i: (i, 0)),
        pl.BlockSpec((1, 1, 6 * d), lambda i: (i // tiles_per_mod, 0, 0)),
        _const_spec((1, d)),
        _const_spec(w_t.shape),
        _const_spec((HEAD_DIM, 1)),
        _const_spec((HEAD_DIM, 1)),
        _const_spec((MLA_KV_RANK, 1)),
        _const_spec(wuk_t.shape),
        _const_spec(wuv_t.shape),
    ]
    args = [tok, mod, g1, w_t, gq, gk, gkv, wuk_t, wuv_t]
    if use_rope:
        tiles_per_seq = rope[0].shape[1] // TM
        for tab in rope:
            in_specs.append(pl.BlockSpec((tab.shape[0], TM), lambda i: (0, i % tiles_per_seq)))
            args.append(tab)
    nh = NA_HEADS * HEAD_DIM
    out_shape = [
        jax.ShapeDtypeStruct((GQA_HEADS, KPAD, t), BF16),
        jax.ShapeDtypeStruct((t, KPAD), BF16),
        jax.ShapeDtypeStruct((nt, GQA_KV_HEADS * HEAD_DIM, TM), BF16),
        jax.ShapeDtypeStruct((NA_HEADS, KPAD, t), BF16),
        jax.ShapeDtypeStruct((t, nh), BF16),
        jax.ShapeDtypeStruct((nt, nh, TM), BF16),
        jax.ShapeDtypeStruct((MLA_HEADS, KPAD, t), BF16),
        jax.ShapeDtypeStruct((t, MLA_HEADS * KPAD), BF16),
        jax.ShapeDtypeStruct((nt, MLA_HEADS * HEAD_DIM, TM), BF16),
    ]
    out_specs = [
        pl.BlockSpec((GQA_HEADS, KPAD, TM), lambda i: (0, 0, i)),
        pl.BlockSpec((TM, KPAD), lambda i: (i, 0)),
        pl.BlockSpec((1, GQA_KV_HEADS * HEAD_DIM, TM), lambda i: (i, 0, 0)),
        pl.BlockSpec((NA_HEADS, KPAD, TM), lambda i: (0, 0, i)),
        pl.BlockSpec((TM, nh), lambda i: (i, 0)),
        pl.BlockSpec((1, nh, TM), lambda i: (i, 0, 0)),
        pl.BlockSpec((MLA_HEADS, KPAD, TM), lambda i: (0, 0, i)),
        pl.BlockSpec((TM, MLA_HEADS * KPAD), lambda i: (i, 0)),
        pl.BlockSpec((1, MLA_HEADS * HEAD_DIM, TM), lambda i: (i, 0, 0)),
    ]
    return pl.pallas_call(
        functools.partial(_proj_kernel, use_rope=use_rope),
        grid=(nt,),
        in_specs=in_specs,
        out_specs=out_specs,
        out_shape=out_shape,
        compiler_params=_cparams(1),
        name="project_rope" if use_rope else "project",
    )(*args)


def _attn_kernel(*refs, n_seg, scale):
    q_ref, o_ref = refs[0], refs[-1]
    q = q_ref[0]
    tq = q.shape[1]
    dv = o_ref.shape[0]
    carry = (jnp.full((1, tq), NEG, F32), jnp.zeros((1, tq), F32), jnp.zeros((dv, tq), F32))
    for s in range(n_seg):
        k_ref, v_ref = refs[1 + 2 * s], refs[2 + 2 * s]
        n_chunks = v_ref.shape[0]

        def body(c, carry, k_ref=k_ref, v_ref=v_ref):
            m, l, acc = carry
            off = pl.multiple_of(c * TM, TM)
            sc = jnp.dot(k_ref[pl.ds(off, TM), :], q, preferred_element_type=F32)
            if scale is not None:
                sc = sc * scale
            m_new = jnp.maximum(m, jnp.max(sc, axis=0, keepdims=True))
            alpha = jnp.exp(m - m_new)
            p = jnp.exp(sc - m_new)
            l = alpha * l + jnp.sum(p, axis=0, keepdims=True)
            acc = alpha * acc + jnp.dot(v_ref[c], p.astype(BF16), preferred_element_type=F32)
            return m_new, l, acc

        if n_chunks == 1:
            carry = body(0, carry)
        else:
            carry = lax.fori_loop(0, n_chunks, body, carry)
    _, l, acc = carry
    o_ref[...] = (acc / l).astype(o_ref.dtype)


def _attention(q, segs, batch, kcol, vrow, scale, tq):
    heads, _, tq_total = q.shape
    sq = tq_total // batch
    nq = sq // tq
    in_specs = [pl.BlockSpec((1, KPAD, tq), lambda b, h, i: (h, 0, b * nq + i))]
    args = [q]
    for k, v in segs:
        n = k.shape[0] // batch
        in_specs.append(pl.BlockSpec((n, KPAD), lambda b, h, i: (b, kcol(h))))
        in_specs.append(pl.BlockSpec((n // TM, HEAD_DIM, TM), lambda b, h, i: (b, vrow(h), 0)))
        args += [k, v]
    return pl.pallas_call(
        functools.partial(_attn_kernel, n_seg=len(segs), scale=scale),
        grid=(batch, heads, nq),
        in_specs=in_specs,
        out_specs=pl.BlockSpec((HEAD_DIM, tq), lambda b, h, i: (h, b * nq + i)),
        out_shape=jax.ShapeDtypeStruct((heads * HEAD_DIM, tq_total), BF16),
        compiler_params=_cparams(3),
        name="attention",
    )(*args)


def _na_kernel(*refs):
    q_ref = refs[0]
    k_refs = refs[1:1 + NA_KBLK + 1]
    v_refs = refs[1 + NA_KBLK + 1:1 + 2 * (NA_KBLK + 1)]
    bias_ref, o_ref = refs[-2], refs[-1]
    q = q_ref[0]
    scores = []
    for j in range(NA_KBLK):
        s = jnp.dot(k_refs[j][...], q, preferred_element_type=F32)
        scores.append(s + bias_ref[0, 0, j * TM:(j + 1) * TM, :])
    scores.append(jnp.dot(k_refs[NA_KBLK][...], q, preferred_element_type=F32))
    m = functools.reduce(jnp.maximum, [jnp.max(s, axis=0, keepdims=True) for s in scores])
    l = None
    acc = None
    for s, v_ref in zip(scores, v_refs):
        p = jnp.exp(s - m)
        ps = jnp.sum(p, axis=0, keepdims=True)
        pv = jnp.dot(v_ref[0], p.astype(BF16), preferred_element_type=F32)
        l = ps if l is None else l + ps
        acc = pv if acc is None else acc + pv
    o_ref[...] = (acc / l).astype(o_ref.dtype)


def _na_bias(rpb, rows):
    heads = rpb.shape[0]
    a = jnp.arange(NA_KBLK * NA_QROWS)
    kc = jnp.arange(GRID_W)
    bq = jnp.arange(NA_QROWS)
    c = jnp.arange(GRID_W)
    cs = jnp.clip(c - NA_KW // 2, 0, GRID_W - NA_KW)
    valid_c = (kc[:, None] >= cs[None, :]) & (kc[:, None] < cs[None, :] + NA_KW)
    ci = jnp.clip(kc[:, None] - c[None, :] + (NA_KW - 1), 0, 2 * NA_KW - 2)
    tiles = []
    for r0 in (0, NA_QROWS, rows - NA_QROWS):
        ks = min(max(r0 - NA_KH // 2, 0), rows - NA_KBLK * NA_QROWS)
        kr = ks + a
        r = r0 + bq
        rs = jnp.clip(r - NA_KH // 2, 0, rows - NA_KH)
        valid_r = (kr[:, None] >= rs[None, :]) & (kr[:, None] < rs[None, :] + NA_KH)
        ri = jnp.clip(kr[:, None] - r[None, :] + (NA_KH - 1), 0, 2 * NA_KH - 2)
        t = rpb[:, ri[:, None, :, None], ci[None, :, None, :]]
        valid = valid_r[:, None, :, None] & valid_c[None, :, None, :]
        t = jnp.where(valid[None], t.astype(F32), NEG)
        tiles.append(t.reshape(heads, NA_KBLK * TM, TM))
    return jnp.stack(tiles, axis=1)


def _na_attention(q, k, v, k_ctx, v_ctx, bias, batch):
    heads, _, t = q.shape
    nq = t // batch // TM
    nctx = k_ctx.shape[0] // batch // TM
    assert nctx == 1 and nq >= NA_KBLK

    def kblk(i):
        return jnp.clip(i - 1, 0, nq - NA_KBLK)

    def case(i):
        return jnp.where(i == 0, 0, jnp.where(i == nq - 1, 2, 1))

    in_specs = [pl.BlockSpec((1, KPAD, TM), lambda b, h, i: (h, 0, b * nq + i))]
    for j in range(NA_KBLK):
        in_specs.append(pl.BlockSpec((TM, KPAD), lambda b, h, i, j=j: (b * nq + kblk(i) + j, h // 2)))
    in_specs.append(pl.BlockSpec((TM, KPAD), lambda b, h, i: (b, h // 2)))
    for j in range(NA_KBLK):
        in_specs.append(pl.BlockSpec((1, HEAD_DIM, TM), lambda b, h, i, j=j: (b * nq + kblk(i) + j, h, 0)))
    in_specs.append(pl.BlockSpec((1, HEAD_DIM, TM), lambda b, h, i: (b, h, 0)))
    in_specs.append(pl.BlockSpec((1, 1, NA_KBLK * TM, TM), lambda b, h, i: (h, case(i), 0, 0)))
    return pl.pallas_call(
        _na_kernel,
        grid=(batch, heads, nq),
        in_specs=in_specs,
        out_specs=pl.BlockSpec((HEAD_DIM, TM), lambda b, h, i: (h, b * nq + i)),
        out_shape=jax.ShapeDtypeStruct((heads * HEAD_DIM, t), BF16),
        compiler_params=_cparams(3),
        name="neighbourhood_attention",
    )(q, k, k, k, k_ctx, v, v, v, v_ctx, bias)


def _merge_kernel(x_ref, mod_ref, g_ref, ya_ref, yb_ref, yc_ref, wg_ref, woa_ref, wob_ref, woc_ref,
                  wout_ref, o_ref):
    d = D_MODEL
    x = x_ref[...]
    m = mod_ref[0]
    h = _rms_mod(x, g_ref[...], m[:, d:2 * d], m[:, 0:d]).astype(BF16)
    y = None
    for i, (y_ref, wo_ref) in enumerate(((ya_ref, woa_ref), (yb_ref, wob_ref), (yc_ref, woc_ref))):
        gate = jax.nn.sigmoid(jnp.dot(h, wg_ref[:, i * d:(i + 1) * d], preferred_element_type=F32))
        yt = y_ref[...].astype(F32).T.astype(BF16)
        u = gate * jnp.dot(yt, wo_ref[...], preferred_element_type=F32)
        y = u if y is None else y + u
    a = jnp.dot(y.astype(BF16), wout_ref[...], preferred_element_type=F32)
    o_ref[...] = x + m[:, 2 * d:3 * d] * a


def _merge(tok, mod, tiles_per_mod, g1, ya, yb, yc, wg, woa, wob, woc, wout):
    t, d = tok.shape
    dy = ya.shape[0]
    y_spec = pl.BlockSpec((dy, TM), lambda i: (0, i))
    return pl.pallas_call(
        _merge_kernel,
        grid=(t // TM,),
        in_specs=[pl.BlockSpec((TM, d), lambda i: (i, 0)),
                  pl.BlockSpec((1, 1, 6 * d), lambda i: (i // tiles_per_mod, 0, 0)),
                  _const_spec((1, d)),
                  y_spec, y_spec, y_spec,
                  _const_spec(wg.shape), _const_spec(woa.shape), _const_spec(wob.shape),
                  _const_spec(woc.shape), _const_spec(wout.shape)],
        out_specs=pl.BlockSpec((TM, d), lambda i: (i, 0)),
        out_shape=jax.ShapeDtypeStruct((t, d), F32),
        compiler_params=_cparams(1),
        name="merge",
    )(tok, mod, g1, ya, yb, yc, wg, woa, wob, woc, wout)


def _mlp_kernel(*refs, final):
    x_ref, mod_ref, g_ref, w1_ref, w2_ref = refs[:5]
    o_ref = refs[-1]
    d = D_MODEL
    x = x_ref[...]
    m = mod_ref[0]
    h = _rms_mod(x, g_ref[...], m[:, 4 * d:5 * d], m[:, 3 * d:4 * d]).astype(BF16)
    a = None
    for c in range(D_FF // d):
        u = jnp.dot(h, w1_ref[:, c * d:(c + 1) * d], preferred_element_type=F32)
        u = jnp.square(jnp.maximum(u, 0.0)).astype(BF16)
        t = jnp.dot(u, w2_ref[c * d:(c + 1) * d, :], preferred_element_type=F32)
        a = t if a is None else a + t
    out = x + m[:, 5 * d:6 * d] * a
    if final:
        gf = refs[5][...]
        ms = jnp.mean(out * out, axis=-1, keepdims=True)
        out = out * lax.rsqrt(ms + EPS) * gf
    o_ref[...] = out


def _mlp(tok, mod, tiles_per_mod, g2, w1, w2, gf):
    t, d = tok.shape
    final = gf is not None
    in_specs = [pl.BlockSpec((TM, d), lambda i: (i, 0)),
                pl.BlockSpec((1, 1, 6 * d), lambda i: (i // tiles_per_mod, 0, 0)),
                _const_spec((1, d)),
                _const_spec(w1.shape), _const_spec(w2.shape)]
    args = [tok, mod, g2, w1, w2]
    if final:
        in_specs.append(_const_spec((1, d)))
        args.append(gf)
    return pl.pallas_call(
        functools.partial(_mlp_kernel, final=final),
        grid=(t // TM,),
        in_specs=in_specs,
        out_specs=pl.BlockSpec((TM, d), lambda i: (i, 0)),
        out_shape=jax.ShapeDtypeStruct((t, d), F32),
        compiler_params=_cparams(1),
        name="mlp_final" if final else "mlp",
    )(*args)


def _rope_tables(n_tokens, rot_dim):
    t = jnp.arange(n_tokens, dtype=jnp.int32)
    row = (t // GRID_W).astype(F32)
    col = (t % GRID_W).astype(F32)
    half = rot_dim // 2
    inv_freq = ROPE_THETA ** (-jnp.arange(0, half, 2, dtype=F32) / half)
    ar, ac = row[:, None] * inv_freq, col[:, None] * inv_freq
    cr, sr, cc, sn = jnp.cos(ar), jnp.sin(ar), jnp.cos(ac), jnp.sin(ac)
    cos = jnp.concatenate([cr, cr, cc, cc], axis=-1).T
    sin = jnp.concatenate([-sr, sr, -sn, sn], axis=-1).T
    return cos, sin


def kernel(x, c, ctx, c_ctx, w_mod, b_mod, norm1_g, norm2_g, w_in, gqa_q_norm, gqa_k_norm, na_rpb,
           mla_kv_norm, mla_w_uk, mla_w_uv, w_o_gqa, w_o_na, w_o_mla, w_out, w_mlp1, w_mlp2,
           final_norm_g):
    batch, seq, d = x.shape
    n_ctx = ctx.shape[1]
    depth = w_mod.shape[0]
    assert d == D_MODEL and seq % (NA_QROWS * GRID_W) == 0 and n_ctx == TM
    rows = seq // GRID_W
    assert rows >= NA_KBLK * NA_QROWS

    cond = jnp.concatenate([c, c_ctx[None, :]], axis=0)
    cond = jnp.pad(cond, ((0, -(batch + 1) % 8), (0, 0)))
    mod_all = _modulation(cond, w_mod, b_mod)

    cos_a, sin_a = _rope_tables(seq, HEAD_DIM)
    cos_m, sin_m = _rope_tables(seq, MLA_ROPE)
    rope = (cos_a, sin_a, cos_m, sin_m)

    xt = x.reshape(batch * seq, d)
    ct = ctx.reshape(batch * n_ctx, d)
    lat_tiles = seq // TM
    ctx_tiles = batch * n_ctx // TM
    group = GQA_HEADS // GQA_KV_HEADS
    sc_m = float(MLA_QK ** -0.5)

    for l in range(depth):
        with_ctx = l < depth - 1
        mod_lat = mod_all[l, :batch].reshape(batch, 1, 6 * d)
        mod_ctx = mod_all[l, batch:batch + 1].reshape(1, 1, 6 * d)
        w_t = w_in[l][:, :O_GATE].T.astype(BF16)
        w_g = w_in[l][:, O_GATE:].astype(BF16)
        g1 = norm1_g[l].reshape(1, d)
        g2 = norm2_g[l].reshape(1, d)
        pargs = (g1, w_t, gqa_q_norm[l].reshape(-1, 1), gqa_k_norm[l].reshape(-1, 1),
                 mla_kv_norm[l].reshape(-1, 1), mla_w_uk[l].T.astype(BF16), mla_w_uv[l].T.astype(BF16))
        qa, ka, va, qn, kn, vn, qm, km, vm = _project(xt, mod_lat, lat_tiles, *pargs, rope)
        cqa, cka, cva, cqn, ckn, cvn, cqm, ckm, cvm = _project(ct, mod_ctx, ctx_tiles, *pargs, None)
        bias = _na_bias(na_rpb[l], rows)

        ya = _attention(qa, [(ka, va), (cka, cva)], batch, lambda h: 0, lambda h: h // group, None, TM)
        yb = _na_attention(qn, kn, vn, ckn, cvn, bias, batch)
        yc = _attention(qm, [(km, vm), (ckm, cvm)], batch, lambda h: h, lambda h: h, sc_m, TM)

        mw = (w_g, w_o_gqa[l].astype(BF16), w_o_na[l].astype(BF16), w_o_mla[l].astype(BF16),
              w_out[l].astype(BF16))
        w1 = w_mlp1[l].astype(BF16)
        w2 = w_mlp2[l].astype(BF16)
        gf = None if with_ctx else final_norm_g.reshape(1, d)
        x1 = _merge(xt, mod_lat, lat_tiles, g1, ya, yb, yc, *mw)
        xt = _mlp(x1, mod_lat, lat_tiles, g2, w1, w2, gf)

        if with_ctx:
            ca = _attention(cqa, [(cka, cva)], batch, lambda h: 0, lambda h: h // group, None, TM)
            cb = _attention(cqn, [(ckn, cvn)], batch, lambda h: h // 2, lambda h: h, None, TM)
            cc = _attention(cqm, [(ckm, cvm)], batch, lambda h: h, lambda h: h, sc_m, TM)
            c1 = _merge(ct, mod_ctx, ctx_tiles, g1, ca, cb, cc, *mw)
            ct = _mlp(c1, mod_ctx, ctx_tiles, g2, w1, w2, None)

    return xt.reshape(batch, seq, d)
```

```python
import functools
import math

import numpy as np
import jax
import jax.numpy as jnp
from jax import lax
from jax.experimental import pallas as pl
from jax.experimental.pallas import tpu as pltpu

F32 = jnp.float32
BF16 = jnp.bfloat16

D_MODEL = 1024
GRID_W = 64
HEAD_DIM = 64
GQA_HEADS = 8
GQA_KV_HEADS = 2
NA_HEADS = 8
NA_KH = 8
NA_KW = 16
MLA_HEADS = 8
MLA_NOPE = 64
MLA_ROPE = 32
MLA_QK = MLA_NOPE + MLA_ROPE
MLA_KV_RANK = 256
D_FF = 4 * D_MODEL
ROPE_THETA = 10000.0
EPS = 1e-6
LOG2E = math.log2(math.e)

O_QA = 0
O_KA = O_QA + GQA_HEADS * HEAD_DIM
O_VA = O_KA + GQA_KV_HEADS * HEAD_DIM
O_QN = O_VA + GQA_KV_HEADS * HEAD_DIM
O_KN = O_QN + NA_HEADS * HEAD_DIM
O_VN = O_KN + NA_HEADS * HEAD_DIM
O_QM = O_VN + NA_HEADS * HEAD_DIM
O_CKV = O_QM + MLA_HEADS * MLA_QK
O_KR = O_CKV + MLA_KV_RANK
O_GATE = O_KR + MLA_ROPE

TM = 256
TQ = 1024
KPAD = 128
ONES_ROWS = 16
NA_QROWS = 4
NA_KBLK = 3
NEG = -1e30
VMEM_LIMIT = 56 * 1024 * 1024

QSCALE_A = HEAD_DIM ** -0.5 * LOG2E
QSCALE_M = MLA_QK ** -0.5 * LOG2E


def _cparams(n_axes):
    return pltpu.CompilerParams(dimension_semantics=("parallel",) * n_axes,
                                vmem_limit_bytes=VMEM_LIMIT)


def _const_spec(shape):
    nd = len(shape)
    return pl.BlockSpec(shape, lambda *_: (0,) * nd, pipeline_mode=pl.Buffered(1))


def _rms_mod(x, g, scale, shift):
    ms = jnp.mean(x * x, axis=-1, keepdims=True)
    return (x * lax.rsqrt(ms + EPS) * g) * (1.0 + scale) + shift


def _rms_fm(x, g):
    ms = jnp.mean(x * x, axis=0, keepdims=True)
    return x * lax.rsqrt(ms + EPS) * g


def _rope_fm(x, cos, sin, blk):
    sw = jnp.concatenate([x[blk:2 * blk], x[0:blk], x[3 * blk:4 * blk], x[2 * blk:3 * blk]], axis=0)
    return x * cos + sw * sin


def _mod_kernel(c_ref, w_ref, b_ref, o_ref):
    c = c_ref[...]
    c = (c * jax.nn.sigmoid(c)).astype(BF16)
    w = w_ref[0].astype(BF16)
    o_ref[0] = jnp.dot(c, w, preferred_element_type=F32) + b_ref[0]


def _modulation(cond, w_mod, b_mod):
    depth, d, n = w_mod.shape
    rows = cond.shape[0]
    tn = 1536
    return pl.pallas_call(
        _mod_kernel,
        grid=(depth, n // tn),
        in_specs=[pl.BlockSpec((rows, d), lambda l, j: (0, 0)),
                  pl.BlockSpec((1, d, tn), lambda l, j: (l, 0, j)),
                  pl.BlockSpec((1, 1, tn), lambda l, j: (l, 0, j))],
        out_specs=pl.BlockSpec((1, rows, tn), lambda l, j: (l, 0, j)),
        out_shape=jax.ShapeDtypeStruct((depth, rows, n), F32),
        compiler_params=_cparams(2),
        name="modulation",
    )(cond, w_mod, b_mod.reshape(depth, 1, n))


def _proj_kernel(x_ref, c_ref, mod_ref, g_ref, w_ref, gq_ref, gk_ref, gkv_ref, wuk_ref, wuv_ref,
                 cosa_ref, sina_ref, cosm_ref, sinm_ref,
                 qa_o, ka_o, va_o, qn_o, kn_o, vn_o, qm_o, km_o, vm_o, *, n_lat):
    d = D_MODEL
    x = jnp.where(pl.program_id(0) < n_lat, x_ref[...], c_ref[...])
    m = mod_ref[0]
    h = _rms_mod(x, g_ref[...], m[:, d:2 * d], m[:, 0:d]).astype(BF16)
    tm = x.shape[0]

    def seg(a, b):
        return lax.dot_general(w_ref[a:b, :], h, (((1,), (1,)), ((), ())),
                               preferred_element_type=F32)

    def rope_a(t):
        return _rope_fm(t, cosa_ref[...], sina_ref[...], HEAD_DIM // 4)

    def rope_m(t):
        return _rope_fm(t, cosm_ref[...], sinm_ref[...], MLA_ROPE // 4)

    zeros64 = jnp.zeros((HEAD_DIM, tm), BF16)

    p = seg(O_QA, O_KA)
    group = GQA_HEADS // GQA_KV_HEADS
    for hh in range(GQA_HEADS):
        q = rope_a(_rms_fm(p[hh * HEAD_DIM:(hh + 1) * HEAD_DIM], gq_ref[...])) * QSCALE_A
        kvh = hh // group
        for j in range(KPAD // HEAD_DIM):
            qa_o[hh, j * HEAD_DIM:(j + 1) * HEAD_DIM, :] = q.astype(BF16) if j == kvh else zeros64

    p = seg(O_KA, O_VA)
    ks = [rope_a(_rms_fm(p[g * HEAD_DIM:(g + 1) * HEAD_DIM], gk_ref[...])) for g in range(GQA_KV_HEADS)]
    ka_o[...] = jnp.concatenate(ks, axis=0).T.astype(BF16)
    va_o[0] = seg(O_VA, O_QN).astype(BF16)

    p = seg(O_QN, O_KN)
    for hh in range(NA_HEADS):
        q = (p[hh * HEAD_DIM:(hh + 1) * HEAD_DIM] * QSCALE_A).astype(BF16)
        for j in range(KPAD // HEAD_DIM):
            qn_o[hh, j * HEAD_DIM:(j + 1) * HEAD_DIM, :] = q if j == hh % 2 else zeros64
    kn_o[...] = seg(O_KN, O_VN).T.astype(BF16)
    vn_o[0] = seg(O_VN, O_QM).astype(BF16)

    p = seg(O_QM, O_CKV)
    zeros_pad = jnp.zeros((KPAD - MLA_QK, tm), BF16)
    for hh in range(MLA_HEADS):
        base = hh * MLA_QK
        qm_o[hh, 0:MLA_NOPE, :] = (p[base:base + MLA_NOPE] * QSCALE_M).astype(BF16)
        qm_o[hh, MLA_NOPE:MLA_QK, :] = (rope_m(p[base + MLA_NOPE:base + MLA_QK]) * QSCALE_M).astype(BF16)
        qm_o[hh, MLA_QK:KPAD, :] = zeros_pad

    ckv = _rms_fm(seg(O_CKV, O_KR), gkv_ref[...]).astype(BF16)
    k_nope = jnp.dot(wuk_ref[...], ckv, preferred_element_type=F32)
    vm_o[0] = jnp.dot(wuv_ref[...], ckv, preferred_element_type=F32).astype(BF16)
    k_rope = rope_m(seg(O_KR, O_GATE))
    zpad = jnp.zeros((KPAD - MLA_QK, tm), F32)
    blocks = []
    for hh in range(MLA_HEADS):
        blocks += [k_nope[hh * MLA_NOPE:(hh + 1) * MLA_NOPE], k_rope, zpad]
    km_o[...] = jnp.concatenate(blocks, axis=0).T.astype(BF16)


def _project(xt, ct, mod, batch, g1, w_t, gq, gk, gkv, wuk_t, wuv_t, rope):
    d = xt.shape[1]
    n_lat = xt.shape[0] // TM
    n_ctx = ct.shape[0] // TM
    assert n_ctx == batch
    tps = n_lat // batch
    nt = n_lat + n_ctx
    t_all = nt * TM

    def is_lat(i):
        return i < n_lat

    def kv_blk(i):
        return jnp.where(is_lat(i), (i // tps) * (tps + 1) + i % tps, (i - n_lat) * (tps + 1) + tps)

    def rope_spec(tab):
        return pl.BlockSpec((tab.shape[0], TM), lambda i: (0, jnp.where(is_lat(i), i % tps, tps)))

    in_specs = [
        pl.BlockSpec((TM, d), lambda i: (jnp.minimum(i, n_lat - 1), 0)),
        pl.BlockSpec((TM, d), lambda i: (jnp.maximum(i - n_lat, 0), 0)),
        pl.BlockSpec((1, 1, 6 * d), lambda i: (jnp.where(is_lat(i), i // tps, batch), 0, 0)),
        _const_spec((1, d)),
        _const_spec(w_t.shape),
        _const_spec((HEAD_DIM, 1)),
        _const_spec((HEAD_DIM, 1)),
        _const_spec((MLA_KV_RANK, 1)),
        _const_spec(wuk_t.shape),
        _const_spec(wuv_t.shape),
    ] + [rope_spec(tab) for tab in rope]
    nh = NA_HEADS * HEAD_DIM
    ga = GQA_KV_HEADS * HEAD_DIM
    out_shape = [
        jax.ShapeDtypeStruct((GQA_HEADS, KPAD, t_all), BF16),
        jax.ShapeDtypeStruct((t_all, KPAD), BF16),
        jax.ShapeDtypeStruct((nt, ga, TM), BF16),
        jax.ShapeDtypeStruct((NA_HEADS, KPAD, t_all), BF16),
        jax.ShapeDtypeStruct((t_all, nh), BF16),
        jax.ShapeDtypeStruct((nt, nh, TM), BF16),
        jax.ShapeDtypeStruct((MLA_HEADS, KPAD, t_all), BF16),
        jax.ShapeDtypeStruct((t_all, MLA_HEADS * KPAD), BF16),
        jax.ShapeDtypeStruct((nt, MLA_HEADS * HEAD_DIM, TM), BF16),
    ]
    out_specs = [
        pl.BlockSpec((GQA_HEADS, KPAD, TM), lambda i: (0, 0, i)),
        pl.BlockSpec((TM, KPAD), lambda i: (kv_blk(i), 0)),
        pl.BlockSpec((1, ga, TM), lambda i: (kv_blk(i), 0, 0)),
        pl.BlockSpec((NA_HEADS, KPAD, TM), lambda i: (0, 0, i)),
        pl.BlockSpec((TM, nh), lambda i: (kv_blk(i), 0)),
        pl.BlockSpec((1, nh, TM), lambda i: (kv_blk(i), 0, 0)),
        pl.BlockSpec((MLA_HEADS, KPAD, TM), lambda i: (0, 0, i)),
        pl.BlockSpec((TM, MLA_HEADS * KPAD), lambda i: (kv_blk(i), 0)),
        pl.BlockSpec((1, MLA_HEADS * HEAD_DIM, TM), lambda i: (kv_blk(i), 0, 0)),
    ]
    return pl.pallas_call(
        functools.partial(_proj_kernel, n_lat=n_lat),
        grid=(nt,),
        in_specs=in_specs,
        out_specs=out_specs,
        out_shape=out_shape,
        compiler_params=_cparams(1),
        name="project",
    )(xt, ct, mod, g1, w_t, gq, gk, gkv, wuk_t, wuv_t, *rope)


def _attn_kernel(q_ref, k_ref, v_ref, o_ref, s_buf, *, n_chunks):
    q = q_ref[0]
    tq = q.shape[1]
    dv = o_ref.shape[0]
    ones = jnp.ones((ONES_ROWS, TM), BF16)

    def produce(c, slot, m_prev):
        off = c * TM if isinstance(c, int) else pl.multiple_of(c * TM, TM)
        s = jnp.dot(k_ref[pl.ds(off, TM), :], q, preferred_element_type=F32)
        s_buf[slot] = s
        m_new = jnp.maximum(m_prev, jnp.max(s, axis=0, keepdims=True))
        return m_new, jnp.exp2(m_prev - m_new)

    def consume(c, slot, m, alpha, acc):
        p = jnp.exp2(s_buf[slot] - m).astype(BF16)
        v_ext = jnp.concatenate([v_ref[c], ones], axis=0)
        return alpha * acc + jnp.dot(v_ext, p, preferred_element_type=F32)

    m, alpha = produce(0, 0, jnp.full((1, tq), NEG, F32))
    acc = jnp.zeros((dv + ONES_ROWS, tq), F32)

    def body(it, carry):
        m, alpha, acc = carry
        c = 2 * it
        m1, a1 = produce(c + 1, 1, m)
        acc = consume(c, 0, m, alpha, acc)
        m2, a2 = produce(c + 2, 0, m1)
        acc = consume(c + 1, 1, m1, a1, acc)
        return m2, a2, acc

    assert n_chunks % 2 == 1
    if n_chunks > 1:
        m, alpha, acc = lax.fori_loop(0, (n_chunks - 1) // 2, body, (m, alpha, acc))
    acc = consume(n_chunks - 1, 0, m, alpha, acc)
    o_ref[...] = (acc[0:dv] / acc[dv:dv + 1]).astype(o_ref.dtype)


def _attention(q, k, v, batch, tps, kcol, vrow, ctx_queries, tq=TM):
    heads = q.shape[0]
    per_b = tps + 1
    if ctx_queries:
        nq, tq, n_chunks = 1, TM, 1
        q_spec = pl.BlockSpec((1, KPAD, TM), lambda b, h, i: (h, 0, batch * tps + b))
        k_spec = pl.BlockSpec((TM, KPAD), lambda b, h, i: (b * per_b + tps, kcol(h)))
        v_spec = pl.BlockSpec((1, HEAD_DIM, TM), lambda b, h, i: (b * per_b + tps, vrow(h), 0))
    else:
        nq, n_chunks = tps * TM // tq, per_b
        q_spec = pl.BlockSpec((1, KPAD, tq), lambda b, h, i: (h, 0, b * nq + i))
        k_spec = pl.BlockSpec((per_b * TM, KPAD), lambda b, h, i: (b, kcol(h)))
        v_spec = pl.BlockSpec((per_b, HEAD_DIM, TM), lambda b, h, i: (b, vrow(h), 0))
    return pl.pallas_call(
        functools.partial(_attn_kernel, n_chunks=n_chunks),
        grid=(batch, heads, nq),
        in_specs=[q_spec, k_spec, v_spec],
        out_specs=pl.BlockSpec((HEAD_DIM, tq), lambda b, h, i: (h, b * nq + i)),
        out_shape=jax.ShapeDtypeStruct((heads * HEAD_DIM, batch * nq * tq), BF16),
        scratch_shapes=[pltpu.VMEM((2, TM, tq), F32)],
        compiler_params=_cparams(3),
        name="ctx_attention" if ctx_queries else "attention",
    )(q, k, v)


def _na_kernel(*refs):
    q_ref = refs[0]
    k_refs = refs[1:1 + NA_KBLK + 1]
    v_refs = refs[1 + NA_KBLK + 1:1 + 2 * (NA_KBLK + 1)]
    bias_ref, o_ref = refs[-2], refs[-1]
    q = q_ref[0]
    scores = []
    for j in range(NA_KBLK):
        s = jnp.dot(k_refs[j][...], q, preferred_element_type=F32)
        scores.append(s + bias_ref[0, 0, j * TM:(j + 1) * TM, :])
    scores.append(jnp.dot(k_refs[NA_KBLK][...], q, preferred_element_type=F32))
    m = functools.reduce(jnp.maximum, [jnp.max(s, axis=0, keepdims=True) for s in scores])
    l = None
    acc = None
    for s, v_ref in zip(scores, v_refs):
        p = jnp.exp2(s - m)
        ps = jnp.sum(p, axis=0, keepdims=True)
        pv = jnp.dot(v_ref[0], p.astype(BF16), preferred_element_type=F32)
        l = ps if l is None else l + ps
        acc = pv if acc is None else acc + pv
    o_ref[...] = (acc / l).astype(o_ref.dtype)


def _na_bias(rpb, rows):
    heads = rpb.shape[0]
    kc = np.arange(GRID_W)[:, None]
    c = np.arange(GRID_W)[None, :]
    cs = np.clip(c - NA_KW // 2, 0, GRID_W - NA_KW)
    valid_c = (kc >= cs) & (kc < cs + NA_KW)
    ci = kc - c + (NA_KW - 1)
    rpb = rpb.astype(F32) * LOG2E
    toe = jnp.full((heads, 2 * NA_KH - 1, GRID_W, GRID_W), NEG, F32)
    for j in range(2 * NA_KW - 1):
        toe = jnp.where(jnp.asarray(valid_c & (ci == j)), rpb[:, :, j][:, :, None, None], toe)
    masked = jnp.full((heads, GRID_W, GRID_W), NEG, F32)
    tiles = []
    for r0 in (0, NA_QROWS, rows - NA_QROWS):
        ks = min(max(r0 - NA_KH // 2, 0), rows - NA_KBLK * NA_QROWS)
        key_rows = []
        for a in range(NA_KBLK * NA_QROWS):
            blocks = []
            for b in range(NA_QROWS):
                kr, r = ks + a, r0 + b
                rs = min(max(r - NA_KH // 2, 0), rows - NA_KH)
                blocks.append(toe[:, kr - r + NA_KH - 1] if rs <= kr < rs + NA_KH else masked)
            key_rows.append(jnp.concatenate(blocks, axis=-1))
        tiles.append(jnp.concatenate(key_rows, axis=1))
    return jnp.stack(tiles, axis=1)


def _na_attention(q, k, v, bias, batch, tps):
    heads = q.shape[0]
    nq = tps
    per_b = tps + 1
    assert nq >= NA_KBLK

    def kblk(i):
        return jnp.clip(i - 1, 0, nq - NA_KBLK)

    def case(i):
        return jnp.where(i == 0, 0, jnp.where(i == nq - 1, 2, 1))

    in_specs = [pl.BlockSpec((1, KPAD, TM), lambda b, h, i: (h, 0, b * nq + i))]
    for j in range(NA_KBLK):
        in_specs.append(pl.BlockSpec((TM, KPAD), lambda b, h, i, j=j: (b * per_b + kblk(i) + j, h // 2)))
    in_specs.append(pl.BlockSpec((TM, KPAD), lambda b, h, i: (b * per_b + tps, h // 2)))
    for j in range(NA_KBLK):
        in_specs.append(pl.BlockSpec((1, HEAD_DIM, TM), lambda b, h, i, j=j: (b * per_b + kblk(i) + j, h, 0)))
    in_specs.append(pl.BlockSpec((1, HEAD_DIM, TM), lambda b, h, i: (b * per_b + tps, h, 0)))
    in_specs.append(pl.BlockSpec((1, 1, NA_KBLK * TM, TM), lambda b, h, i: (h, case(i), 0, 0)))
    return pl.pallas_call(
        _na_kernel,
        grid=(batch, heads, nq),
        in_specs=in_specs,
        out_specs=pl.BlockSpec((HEAD_DIM, TM), lambda b, h, i: (h, b * nq + i)),
        out_shape=jax.ShapeDtypeStruct((heads * HEAD_DIM, batch * nq * TM), BF16),
        compiler_params=_cparams(3),
        name="neighbourhood_attention",
    )(q, k, k, k, k, v, v, v, v, bias)


def _merge_kernel(x_ref, mod_ref, g_ref, ya_ref, yb_ref, yc_ref, wg_ref, woa_ref, wob_ref, woc_ref,
                  wout_ref, o_ref):
    d = D_MODEL
    x = x_ref[...]
    m = mod_ref[0]
    h = _rms_mod(x, g_ref[...], m[:, d:2 * d], m[:, 0:d]).astype(BF16)
    y = None
    for i, (y_ref, wo_ref) in enumerate(((ya_ref, woa_ref), (yb_ref, wob_ref), (yc_ref, woc_ref))):
        gate = jax.nn.sigmoid(jnp.dot(h, wg_ref[:, i * d:(i + 1) * d], preferred_element_type=F32))
        yt = y_ref[...].astype(F32).T.astype(BF16)
        u = gate * jnp.dot(yt, wo_ref[...], preferred_element_type=F32)
        y = u if y is None else y + u
    a = jnp.dot(y.astype(BF16), wout_ref[...], preferred_element_type=F32)
    o_ref[...] = x + m[:, 2 * d:3 * d] * a


def _merge(tok, mod, mod_row, g1, ya, yb, yc, wg, woa, wob, woc, wout):
    t, d = tok.shape
    dy = ya.shape[0]
    y_spec = pl.BlockSpec((dy, TM), lambda i: (0, i))
    return pl.pallas_call(
        _merge_kernel,
        grid=(t // TM,),
        in_specs=[pl.BlockSpec((TM, d), lambda i: (i, 0)),
                  pl.BlockSpec((1, 1, 6 * d), lambda i: (mod_row(i), 0, 0)),
                  _const_spec((1, d)),
                  y_spec, y_spec, y_spec,
                  _const_spec(wg.shape), _const_spec(woa.shape), _const_spec(wob.shape),
                  _const_spec(woc.shape), _const_spec(wout.shape)],
        out_specs=pl.BlockSpec((TM, d), lambda i: (i, 0)),
        out_shape=jax.ShapeDtypeStruct((t, d), F32),
        compiler_params=_cparams(1),
        name="merge",
    )(tok, mod, g1, ya, yb, yc, wg, woa, wob, woc, wout)


def _mlp_kernel(*refs, final):
    x_ref, mod_ref, g_ref, w1_ref, w2_ref = refs[:5]
    o_ref = refs[-1]
    d = D_MODEL
    x = x_ref[...]
    m = mod_ref[0]
    h = _rms_mod(x, g_ref[...], m[:, 4 * d:5 * d], m[:, 3 * d:4 * d]).astype(BF16)
    a = None
    for c in range(D_FF // d):
        u = jnp.dot(h, w1_ref[:, c * d:(c + 1) * d], preferred_element_type=F32)
        u = jnp.square(jnp.maximum(u, 0.0)).astype(BF16)
        t = jnp.dot(u, w2_ref[c * d:(c + 1) * d, :], preferred_element_type=F32)
        a = t if a is None else a + t
    out = x + m[:, 5 * d:6 * d] * a
    if final:
        gf = refs[5][...]
        ms = jnp.mean(out * out, axis=-1, keepdims=True)
        out = out * lax.rsqrt(ms + EPS) * gf
    o_ref[...] = out


def _mlp(tok, mod, mod_row, g2, w1, w2, gf):
    t, d = tok.shape
    final = gf is not None
    in_specs = [pl.BlockSpec((TM, d), lambda i: (i, 0)),
                pl.BlockSpec((1, 1, 6 * d), lambda i: (mod_row(i), 0, 0)),
                _const_spec((1, d)),
                _const_spec(w1.shape), _const_spec(w2.shape)]
    args = [tok, mod, g2, w1, w2]
    if final:
        in_specs.append(_const_spec((1, d)))
        args.append(gf)
    return pl.pallas_call(
        functools.partial(_mlp_kernel, final=final),
        grid=(t // TM,),
        in_specs=in_specs,
        out_specs=pl.BlockSpec((TM, d), lambda i: (i, 0)),
        out_shape=jax.ShapeDtypeStruct((t, d), F32),
        compiler_params=_cparams(1),
        name="mlp_final" if final else "mlp",
    )(*args)


def _rope_tables(n_tokens, rot_dim):
    t = jnp.arange(n_tokens, dtype=jnp.int32)
    row = (t // GRID_W).astype(F32)
    col = (t % GRID_W).astype(F32)
    half = rot_dim // 2
    inv_freq = ROPE_THETA ** (-jnp.arange(0, half, 2, dtype=F32) / half)
    ar, ac = row[:, None] * inv_freq, col[:, None] * inv_freq
    cr, sr, cc, sn = jnp.cos(ar), jnp.sin(ar), jnp.cos(ac), jnp.sin(ac)
    cos = jnp.concatenate([cr, cr, cc, cc], axis=-1).T
    sin = jnp.concatenate([-sr, sr, -sn, sn], axis=-1).T
    cos = jnp.concatenate([cos, jnp.ones((rot_dim, TM), F32)], axis=1)
    sin = jnp.concatenate([sin, jnp.zeros((rot_dim, TM), F32)], axis=1)
    return cos, sin


def kernel(x, c, ctx, c_ctx, w_mod, b_mod, norm1_g, norm2_g, w_in, gqa_q_norm, gqa_k_norm, na_rpb,
           mla_kv_norm, mla_w_uk, mla_w_uv, w_o_gqa, w_o_na, w_o_mla, w_out, w_mlp1, w_mlp2,
           final_norm_g):
    batch, seq, d = x.shape
    n_ctx = ctx.shape[1]
    depth = w_mod.shape[0]
    assert d == D_MODEL and seq % (NA_QROWS * GRID_W) == 0 and n_ctx == TM
    rows = seq // GRID_W
    assert rows >= NA_KBLK * NA_QROWS
    tps = seq // TM
    tq = min(TQ, seq)

    cond = jnp.concatenate([c, c_ctx[None, :]], axis=0)
    cond = jnp.pad(cond, ((0, -(batch + 1) % 8), (0, 0)))
    mod_all = _modulation(cond, w_mod, b_mod)

    rope = _rope_tables(seq, HEAD_DIM) + _rope_tables(seq, MLA_ROPE)

    xt = x.reshape(batch * seq, d)
    ct = ctx.reshape(batch * n_ctx, d)
    group = GQA_HEADS // GQA_KV_HEADS

    def lat_row(i):
        return i // tps

    def ctx_row(i):
        return batch

    for l in range(depth):
        with_ctx = l < depth - 1
        mod = mod_all[l, :batch + 1].reshape(batch + 1, 1, 6 * d)
        w_t = w_in[l][:, :O_GATE].T.astype(BF16)
        w_g = w_in[l][:, O_GATE:].astype(BF16)
        g1 = norm1_g[l].reshape(1, d)
        g2 = norm2_g[l].reshape(1, d)
        qa, ka, va, qn, kn, vn, qm, km, vm = _project(
            xt, ct, mod, batch, g1, w_t, gqa_q_norm[l].reshape(-1, 1), gqa_k_norm[l].reshape(-1, 1),
            mla_kv_norm[l].reshape(-1, 1), mla_w_uk[l].T.astype(BF16), mla_w_uv[l].T.astype(BF16), rope)
        bias = _na_bias(na_rpb[l], rows)

        att_a = functools.partial(_attention, qa, ka, va, batch, tps, lambda h: 0, lambda h: h // group)
        att_n = functools.partial(_attention, qn, kn, vn, batch, tps, lambda h: h // 2, lambda h: h)
        att_m = functools.partial(_attention, qm, km, vm, batch, tps, lambda h: h, lambda h: h)
        ya = att_a(False, tq)
        yb = _na_attention(qn, kn, vn, bias, batch, tps)
        yc = att_m(False, tq)

        mw = (w_g, w_o_gqa[l].astype(BF16), w_o_na[l].astype(BF16), w_o_mla[l].astype(BF16),
              w_out[l].astype(BF16))
        w1 = w_mlp1[l].astype(BF16)
        w2 = w_mlp2[l].astype(BF16)
        gf = None if with_ctx else final_norm_g.reshape(1, d)
        x1 = _merge(xt, mod, lat_row, g1, ya, yb, yc, *mw)
        xt = _mlp(x1, mod, lat_row, g2, w1, w2, gf)

        if with_ctx:
            c1 = _merge(ct, mod, ctx_row, g1, att_a(True), att_n(True), att_m(True), *mw)
            ct = _mlp(c1, mod, ctx_row, g2, w1, w2, None)

    return xt.reshape(batch, seq, d)
```

```python
import functools
import math

import numpy as np
import jax
import jax.numpy as jnp
from jax import lax
from jax.experimental import pallas as pl
from jax.experimental.pallas import tpu as pltpu

F32 = jnp.float32
BF16 = jnp.bfloat16

D_MODEL = 1024
GRID_W = 64
HEAD_DIM = 64
GQA_HEADS = 8
GQA_KV_HEADS = 2
NA_HEADS = 8
NA_KH = 8
NA_KW = 16
MLA_HEADS = 8
MLA_NOPE = 64
MLA_ROPE = 32
MLA_QK = MLA_NOPE + MLA_ROPE
MLA_KV_RANK = 256
D_FF = 4 * D_MODEL
ROPE_THETA = 10000.0
EPS = 1e-6
LOG2E = math.log2(math.e)

O_QA = 0
O_KA = O_QA + GQA_HEADS * HEAD_DIM
O_VA = O_KA + GQA_KV_HEADS * HEAD_DIM
O_QN = O_VA + GQA_KV_HEADS * HEAD_DIM
O_KN = O_QN + NA_HEADS * HEAD_DIM
O_VN = O_KN + NA_HEADS * HEAD_DIM
O_QM = O_VN + NA_HEADS * HEAD_DIM
O_CKV = O_QM + MLA_HEADS * MLA_QK
O_KR = O_CKV + MLA_KV_RANK
O_GATE = O_KR + MLA_ROPE

TM = 256
TQ = 1024
ATTN_UNROLL = 16
KPAD = 128
ONES_ROWS = 16
NA_QROWS = 4
NA_UNROLL = 10
NA_KBLK = 3
NEG = -1e30
VMEM_LIMIT = 56 * 1024 * 1024

QSCALE_A = HEAD_DIM ** -0.5 * LOG2E
QSCALE_M = MLA_QK ** -0.5 * LOG2E


def _cparams(n_axes):
    return pltpu.CompilerParams(dimension_semantics=("parallel",) * n_axes,
                                vmem_limit_bytes=VMEM_LIMIT)


def _const_spec(shape):
    nd = len(shape)
    return pl.BlockSpec(shape, lambda *_: (0,) * nd, pipeline_mode=pl.Buffered(1))


def _rms_mod(x, g, scale, shift):
    ms = jnp.mean(x * x, axis=-1, keepdims=True)
    return (x * lax.rsqrt(ms + EPS) * g) * (1.0 + scale) + shift


def _rms_fm(x, g):
    ms = jnp.mean(x * x, axis=0, keepdims=True)
    return x * lax.rsqrt(ms + EPS) * g


def _rope_fm(x, cos, sin, blk):
    sw = jnp.concatenate([x[blk:2 * blk], x[0:blk], x[3 * blk:4 * blk], x[2 * blk:3 * blk]], axis=0)
    return x * cos + sw * sin


def _mod_kernel(c_ref, w_ref, b_ref, o_ref):
    c = c_ref[...]
    c = (c * jax.nn.sigmoid(c)).astype(BF16)
    w = w_ref[0].astype(BF16)
    o_ref[0] = jnp.dot(c, w, preferred_element_type=F32) + b_ref[0]


def _modulation(cond, w_mod, b_mod):
    depth, d, n = w_mod.shape
    rows = cond.shape[0]
    tn = 1536
    return pl.pallas_call(
        _mod_kernel,
        grid=(depth, n // tn),
        in_specs=[pl.BlockSpec((rows, d), lambda l, j: (0, 0)),
                  pl.BlockSpec((1, d, tn), lambda l, j: (l, 0, j)),
                  pl.BlockSpec((1, 1, tn), lambda l, j: (l, 0, j))],
        out_specs=pl.BlockSpec((1, rows, tn), lambda l, j: (l, 0, j)),
        out_shape=jax.ShapeDtypeStruct((depth, rows, n), F32),
        compiler_params=_cparams(2),
        name="modulation",
    )(cond, w_mod, b_mod.reshape(depth, 1, n))


def _proj_kernel(x_ref, c_ref, mod_ref, g_ref, w_ref, gq_ref, gk_ref, gkv_ref, wuk_ref, wuv_ref,
                 cosa_ref, sina_ref, cosm_ref, sinm_ref,
                 qa_o, ka_o, va_o, qn_o, kn_o, vn_o, qm_o, km_o, vm_o, *, n_lat):
    d = D_MODEL
    x = jnp.where(pl.program_id(0) < n_lat, x_ref[...], c_ref[...])
    m = mod_ref[0]
    h = _rms_mod(x, g_ref[...], m[:, d:2 * d], m[:, 0:d]).astype(BF16)
    tm = x.shape[0]

    def seg(a, b):
        return lax.dot_general(w_ref[a:b, :], h, (((1,), (1,)), ((), ())),
                               preferred_element_type=F32)

    def rope_a(t):
        return _rope_fm(t, cosa_ref[...], sina_ref[...], HEAD_DIM // 4)

    def rope_m(t):
        return _rope_fm(t, cosm_ref[...], sinm_ref[...], MLA_ROPE // 4)

    zeros64 = jnp.zeros((HEAD_DIM, tm), BF16)

    p = seg(O_QA, O_KA)
    group = GQA_HEADS // GQA_KV_HEADS
    for hh in range(GQA_HEADS):
        q = rope_a(_rms_fm(p[hh * HEAD_DIM:(hh + 1) * HEAD_DIM], gq_ref[...])) * QSCALE_A
        kvh = hh // group
        for j in range(KPAD // HEAD_DIM):
            qa_o[hh, j * HEAD_DIM:(j + 1) * HEAD_DIM, :] = q.astype(BF16) if j == kvh else zeros64

    p = seg(O_KA, O_VA)
    ks = [rope_a(_rms_fm(p[g * HEAD_DIM:(g + 1) * HEAD_DIM], gk_ref[...])) for g in range(GQA_KV_HEADS)]
    ka_o[...] = jnp.concatenate(ks, axis=0).T.astype(BF16)
    va_o[0] = seg(O_VA, O_QN).astype(BF16)

    p = seg(O_QN, O_KN)
    for hh in range(NA_HEADS):
        q = (p[hh * HEAD_DIM:(hh + 1) * HEAD_DIM] * QSCALE_A).astype(BF16)
        for j in range(KPAD // HEAD_DIM):
            r0 = hh * KPAD + j * HEAD_DIM
            qn_o[0, r0:r0 + HEAD_DIM, :] = q if j == hh % 2 else zeros64
    kn_o[...] = seg(O_KN, O_VN).T.astype(BF16)
    vn_o[0] = seg(O_VN, O_QM).astype(BF16)

    p = seg(O_QM, O_CKV)
    zeros_pad = jnp.zeros((KPAD - MLA_QK, tm), BF16)
    for hh in range(MLA_HEADS):
        base = hh * MLA_QK
        qm_o[hh, 0:MLA_NOPE, :] = (p[base:base + MLA_NOPE] * QSCALE_M).astype(BF16)
        qm_o[hh, MLA_NOPE:MLA_QK, :] = (rope_m(p[base + MLA_NOPE:base + MLA_QK]) * QSCALE_M).astype(BF16)
        qm_o[hh, MLA_QK:KPAD, :] = zeros_pad

    ckv = _rms_fm(seg(O_CKV, O_KR), gkv_ref[...]).astype(BF16)
    k_nope = jnp.dot(wuk_ref[...], ckv, preferred_element_type=F32)
    vm_o[0] = jnp.dot(wuv_ref[...], ckv, preferred_element_type=F32).astype(BF16)
    k_rope = rope_m(seg(O_KR, O_GATE))
    zpad = jnp.zeros((KPAD - MLA_QK, tm), F32)
    blocks = []
    for hh in range(MLA_HEADS):
        blocks += [k_nope[hh * MLA_NOPE:(hh + 1) * MLA_NOPE], k_rope, zpad]
    km_o[...] = jnp.concatenate(blocks, axis=0).T.astype(BF16)


def _project(xt, ct, mod, batch, g1, w_t, gq, gk, gkv, wuk_t, wuv_t, rope):
    d = xt.shape[1]
    n_lat = xt.shape[0] // TM
    n_ctx = ct.shape[0] // TM
    assert n_ctx == batch
    tps = n_lat // batch
    nt = n_lat + n_ctx
    t_all = nt * TM

    def is_lat(i):
        return i < n_lat

    def kv_blk(i):
        return jnp.where(is_lat(i), (i // tps) * (tps + 1) + i % tps, (i - n_lat) * (tps + 1) + tps)

    def rope_spec(tab):
        return pl.BlockSpec((tab.shape[0], TM), lambda i: (0, jnp.where(is_lat(i), i % tps, tps)))

    in_specs = [
        pl.BlockSpec((TM, d), lambda i: (jnp.minimum(i, n_lat - 1), 0)),
        pl.BlockSpec((TM, d), lambda i: (jnp.maximum(i - n_lat, 0), 0)),
        pl.BlockSpec((1, 1, 6 * d), lambda i: (jnp.where(is_lat(i), i // tps, batch), 0, 0)),
        _const_spec((1, d)),
        _const_spec(w_t.shape),
        _const_spec((HEAD_DIM, 1)),
        _const_spec((HEAD_DIM, 1)),
        _const_spec((MLA_KV_RANK, 1)),
        _const_spec(wuk_t.shape),
        _const_spec(wuv_t.shape),
    ] + [rope_spec(tab) for tab in rope]
    nh = NA_HEADS * HEAD_DIM
    ga = GQA_KV_HEADS * HEAD_DIM
    out_shape = [
        jax.ShapeDtypeStruct((GQA_HEADS, KPAD, t_all), BF16),
        jax.ShapeDtypeStruct((t_all, KPAD), BF16),
        jax.ShapeDtypeStruct((nt, ga, TM), BF16),
        jax.ShapeDtypeStruct((nt, NA_HEADS * KPAD, TM), BF16),
        jax.ShapeDtypeStruct((t_all, nh), BF16),
        jax.ShapeDtypeStruct((nt, nh, TM), BF16),
        jax.ShapeDtypeStruct((MLA_HEADS, KPAD, t_all), BF16),
        jax.ShapeDtypeStruct((t_all, MLA_HEADS * KPAD), BF16),
        jax.ShapeDtypeStruct((nt, MLA_HEADS * HEAD_DIM, TM), BF16),
    ]
    out_specs = [
        pl.BlockSpec((GQA_HEADS, KPAD, TM), lambda i: (0, 0, i)),
        pl.BlockSpec((TM, KPAD), lambda i: (kv_blk(i), 0)),
        pl.BlockSpec((1, ga, TM), lambda i: (kv_blk(i), 0, 0)),
        pl.BlockSpec((1, NA_HEADS * KPAD, TM), lambda i: (i, 0, 0)),
        pl.BlockSpec((TM, nh), lambda i: (kv_blk(i), 0)),
        pl.BlockSpec((1, nh, TM), lambda i: (kv_blk(i), 0, 0)),
        pl.BlockSpec((MLA_HEADS, KPAD, TM), lambda i: (0, 0, i)),
        pl.BlockSpec((TM, MLA_HEADS * KPAD), lambda i: (kv_blk(i), 0)),
        pl.BlockSpec((1, MLA_HEADS * HEAD_DIM, TM), lambda i: (kv_blk(i), 0, 0)),
    ]
    return pl.pallas_call(
        functools.partial(_proj_kernel, n_lat=n_lat),
        grid=(nt,),
        in_specs=in_specs,
        out_specs=out_specs,
        out_shape=out_shape,
        compiler_params=_cparams(1),
        name="project",
    )(xt, ct, mod, g1, w_t, gq, gk, gkv, wuk_t, wuv_t, *rope)


def _attn_kernel(q_ref, k_ref, v_ref, o_ref, s_buf, *, n_chunks):
    q = q_ref[0]
    tq = q.shape[1]
    dv = o_ref.shape[1]
    ones = jnp.ones((ONES_ROWS, TM), BF16)

    def produce(c, slot, m_prev):
        off = c * TM if isinstance(c, int) else pl.multiple_of(c * TM, TM)
        s = jnp.dot(k_ref[pl.ds(off, TM), :], q, preferred_element_type=F32)
        s_buf[slot] = s
        m_new = jnp.maximum(m_prev, jnp.max(s, axis=0, keepdims=True))
        return m_new, jnp.exp2(m_prev - m_new)

    def consume(c, slot, m, alpha, acc):
        p = jnp.exp2(s_buf[slot] - m).astype(BF16)
        v_ext = jnp.concatenate([v_ref[c], ones], axis=0)
        return alpha * acc + jnp.dot(v_ext, p, preferred_element_type=F32)

    m, alpha = produce(0, 0, jnp.full((1, tq), NEG, F32))
    acc = jnp.zeros((dv + ONES_ROWS, tq), F32)

    unroll = math.gcd(n_chunks - 1, ATTN_UNROLL)
    assert n_chunks == 1 or unroll % 2 == 0

    def body(it, carry):
        m, alpha, acc = carry
        c = unroll * it
        for j in range(unroll):
            m_next, a_next = produce(c + j + 1, (j + 1) % 2, m)
            acc = consume(c + j, j % 2, m, alpha, acc)
            m, alpha = m_next, a_next
        return m, alpha, acc

    if n_chunks > 1:
        m, alpha, acc = lax.fori_loop(0, (n_chunks - 1) // unroll, body, (m, alpha, acc))
    acc = consume(n_chunks - 1, 0, m, alpha, acc)
    out = (acc[0:dv] / acc[dv:dv + 1]).astype(o_ref.dtype)
    for j in range(tq // TM):
        o_ref[j] = out[:, j * TM:(j + 1) * TM]


def _attention(q, k, v, batch, tps, kcol, vrow, ctx_queries, tq=TM, q_chunked=False):
    heads = q.shape[1] // KPAD if q_chunked else q.shape[0]
    per_b = tps + 1
    if ctx_queries:
        nq, tq, n_chunks = 1, TM, 1
        if q_chunked:
            q_spec = pl.BlockSpec((1, KPAD, TM), lambda b, h, i: (batch * tps + b, h, 0))
        else:
            q_spec = pl.BlockSpec((1, KPAD, TM), lambda b, h, i: (h, 0, batch * tps + b))
        k_spec = pl.BlockSpec((TM, KPAD), lambda b, h, i: (b * per_b + tps, kcol(h)))
        v_spec = pl.BlockSpec((1, HEAD_DIM, TM), lambda b, h, i: (b * per_b + tps, vrow(h), 0))
    else:
        assert not q_chunked
        nq, n_chunks = tps * TM // tq, per_b
        q_spec = pl.BlockSpec((1, KPAD, tq), lambda b, h, i: (h, 0, b * nq + i))
        k_spec = pl.BlockSpec((per_b * TM, KPAD), lambda b, h, i: (b, kcol(h)))
        v_spec = pl.BlockSpec((per_b, HEAD_DIM, TM), lambda b, h, i: (b, vrow(h), 0))
    tiles = tq // TM
    return pl.pallas_call(
        functools.partial(_attn_kernel, n_chunks=n_chunks),
        grid=(batch, heads, nq),
        in_specs=[q_spec, k_spec, v_spec],
        out_specs=pl.BlockSpec((tiles, HEAD_DIM, TM), lambda b, h, i: (b * nq + i, h, 0)),
        out_shape=jax.ShapeDtypeStruct((batch * nq * tiles, heads * HEAD_DIM, TM), BF16),
        scratch_shapes=[pltpu.VMEM((2, TM, tq), F32)],
        compiler_params=_cparams(3),
        name="ctx_attention" if ctx_queries else "attention",
    )(q, k, v)


def _na_kernel(q_ref, k_ref, v_ref, bias_ref, o_ref, s_buf, *, nq):
    win = NA_KBLK * TM
    ones = jnp.ones((ONES_ROWS, TM), BF16)
    static = lambda i: isinstance(i, int)

    def kblk(i):
        return min(max(i - 1, 0), nq - NA_KBLK) if static(i) else jnp.clip(i - 1, 0, nq - NA_KBLK)

    def produce(i, slot):
        if static(i):
            case, off = (0 if i == 0 else 2 if i == nq - 1 else 1), kblk(i) * TM
        else:
            case = jnp.where(i == 0, 0, jnp.where(i == nq - 1, 2, 1))
            off = pl.multiple_of(kblk(i) * TM, TM)
        q = q_ref[i]
        s_win = jnp.dot(k_ref[pl.ds(off, win), :], q, preferred_element_type=F32) + bias_ref[0, case]
        s_ctx = jnp.dot(k_ref[pl.ds(nq * TM, TM), :], q, preferred_element_type=F32)
        s_buf[slot, 0:win, :] = s_win
        s_buf[slot, win:win + TM, :] = s_ctx
        return jnp.maximum(jnp.max(s_win, axis=0, keepdims=True), jnp.max(s_ctx, axis=0, keepdims=True))

    def consume(i, slot, m):
        p = jnp.exp2(s_buf[slot] - m).astype(BF16)
        kb = kblk(i)
        chunks = [v_ref[kb + j] for j in range(NA_KBLK)] + [v_ref[nq]]
        v_all = jnp.concatenate([jnp.concatenate([c, ones], axis=0) for c in chunks], axis=1)
        acc = jnp.dot(v_all, p, preferred_element_type=F32)
        dv = o_ref.shape[1]
        o_ref[i] = (acc[0:dv] / acc[dv:dv + 1]).astype(o_ref.dtype)

    unroll = max(u for u in range(2, NA_UNROLL + 1, 2) if (nq - 2) % u == 0)
    m = produce(0, 0)

    def body(it, m):
        c = unroll * it
        for j in range(unroll):
            m_next = produce(c + j + 1, (j + 1) % 2)
            consume(c + j, j % 2, m)
            m = m_next
        return m

    m = lax.fori_loop(0, (nq - 2) // unroll, body, m)
    m_last = produce(nq - 1, (nq - 1) % 2)
    consume(nq - 2, (nq - 2) % 2, m)
    consume(nq - 1, (nq - 1) % 2, m_last)


def _na_bias(rpb, rows):
    heads = rpb.shape[0]
    kc = np.arange(GRID_W)[:, None]
    c = np.arange(GRID_W)[None, :]
    cs = np.clip(c - NA_KW // 2, 0, GRID_W - NA_KW)
    valid_c = (kc >= cs) & (kc < cs + NA_KW)
    ci = kc - c + (NA_KW - 1)
    rpb = rpb.astype(F32) * LOG2E
    toe = jnp.full((heads, 2 * NA_KH - 1, GRID_W, GRID_W), NEG, F32)
    for j in range(2 * NA_KW - 1):
        toe = jnp.where(jnp.asarray(valid_c & (ci == j)), rpb[:, :, j][:, :, None, None], toe)
    masked = jnp.full((heads, GRID_W, GRID_W), NEG, F32)
    tiles = []
    for r0 in (0, NA_QROWS, rows - NA_QROWS):
        ks = min(max(r0 - NA_KH // 2, 0), rows - NA_KBLK * NA_QROWS)
        key_rows = []
        for a in range(NA_KBLK * NA_QROWS):
            blocks = []
            for b in range(NA_QROWS):
                kr, r = ks + a, r0 + b
                rs = min(max(r - NA_KH // 2, 0), rows - NA_KH)
                blocks.append(toe[:, kr - r + NA_KH - 1] if rs <= kr < rs + NA_KH else masked)
            key_rows.append(jnp.concatenate(blocks, axis=-1))
        tiles.append(jnp.concatenate(key_rows, axis=1))
    return jnp.stack(tiles, axis=1)


def _na_attention(q, k, v, bias, batch, tps):
    heads = q.shape[1] // KPAD
    nq = tps
    per_b = tps + 1
    assert nq >= NA_KBLK and nq % 2 == 0
    return pl.pallas_call(
        functools.partial(_na_kernel, nq=nq),
        grid=(batch, heads),
        in_specs=[pl.BlockSpec((nq, KPAD, TM), lambda b, h: (b, h, 0)),
                  pl.BlockSpec((per_b * TM, KPAD), lambda b, h: (b, h // 2)),
                  pl.BlockSpec((per_b, HEAD_DIM, TM), lambda b, h: (b, h, 0)),
                  pl.BlockSpec((1, 3, NA_KBLK * TM, TM), lambda b, h: (h, 0, 0, 0))],
        out_specs=pl.BlockSpec((nq, HEAD_DIM, TM), lambda b, h: (b, h, 0)),
        out_shape=jax.ShapeDtypeStruct((batch * nq, heads * HEAD_DIM, TM), BF16),
        scratch_shapes=[pltpu.VMEM((2, (NA_KBLK + 1) * TM, TM), F32)],
        compiler_params=_cparams(2),
        name="neighbourhood_attention",
    )(q, k, v, bias)


def _merge_kernel(x_ref, mod_ref, g_ref, ya_ref, yb_ref, yc_ref, wg_ref, woa_ref, wob_ref, woc_ref,
                  wout_ref, o_ref):
    d = D_MODEL
    x = x_ref[...]
    m = mod_ref[0]
    h = _rms_mod(x, g_ref[...], m[:, d:2 * d], m[:, 0:d]).astype(BF16)
    y = None
    for i, (y_ref, wo_ref) in enumerate(((ya_ref, woa_ref), (yb_ref, wob_ref), (yc_ref, woc_ref))):
        gate = jax.nn.sigmoid(jnp.dot(h, wg_ref[:, i * d:(i + 1) * d], preferred_element_type=F32))
        yt = y_ref[0].astype(F32).T.astype(BF16)
        u = gate * jnp.dot(yt, wo_ref[...], preferred_element_type=F32)
        y = u if y is None else y + u
    a = jnp.dot(y.astype(BF16), wout_ref[...], preferred_element_type=F32)
    o_ref[...] = x + m[:, 2 * d:3 * d] * a


def _merge(tok, mod, mod_row, g1, ya, yb, yc, wg, woa, wob, woc, wout):
    t, d = tok.shape
    dy = ya.shape[1]
    y_spec = pl.BlockSpec((1, dy, TM), lambda i: (i, 0, 0))
    return pl.pallas_call(
        _merge_kernel,
        grid=(t // TM,),
        in_specs=[pl.BlockSpec((TM, d), lambda i: (i, 0)),
                  pl.BlockSpec((1, 1, 6 * d), lambda i: (mod_row(i), 0, 0)),
                  _const_spec((1, d)),
                  y_spec, y_spec, y_spec,
                  _const_spec(wg.shape), _const_spec(woa.shape), _const_spec(wob.shape),
                  _const_spec(woc.shape), _const_spec(wout.shape)],
        out_specs=pl.BlockSpec((TM, d), lambda i: (i, 0)),
        out_shape=jax.ShapeDtypeStruct((t, d), F32),
        compiler_params=_cparams(1),
        name="merge",
    )(tok, mod, g1, ya, yb, yc, wg, woa, wob, woc, wout)


def _mlp_kernel(*refs, final):
    x_ref, mod_ref, g_ref, w1_ref, w2_ref = refs[:5]
    o_ref = refs[-1]
    d = D_MODEL
    x = x_ref[...]
    m = mod_ref[0]
    h = _rms_mod(x, g_ref[...], m[:, 4 * d:5 * d], m[:, 3 * d:4 * d]).astype(BF16)
    a = None
    for c in range(D_FF // d):
        u = jnp.dot(h, w1_ref[:, c * d:(c + 1) * d], preferred_element_type=F32)
        u = jnp.square(jnp.maximum(u, 0.0)).astype(BF16)
        t = jnp.dot(u, w2_ref[c * d:(c + 1) * d, :], preferred_element_type=F32)
        a = t if a is None else a + t
    out = x + m[:, 5 * d:6 * d] * a
    if final:
        gf = refs[5][...]
        ms = jnp.mean(out * out, axis=-1, keepdims=True)
        out = out * lax.rsqrt(ms + EPS) * gf
    o_ref[...] = out


def _mlp(tok, mod, mod_row, g2, w1, w2, gf):
    t, d = tok.shape
    final = gf is not None
    in_specs = [pl.BlockSpec((TM, d), lambda i: (i, 0)),
                pl.BlockSpec((1, 1, 6 * d), lambda i: (mod_row(i), 0, 0)),
                _const_spec((1, d)),
                _const_spec(w1.shape), _const_spec(w2.shape)]
    args = [tok, mod, g2, w1, w2]
    if final:
        in_specs.append(_const_spec((1, d)))
        args.append(gf)
    return pl.pallas_call(
        functools.partial(_mlp_kernel, final=final),
        grid=(t // TM,),
        in_specs=in_specs,
        out_specs=pl.BlockSpec((TM, d), lambda i: (i, 0)),
        out_shape=jax.ShapeDtypeStruct((t, d), F32),
        compiler_params=_cparams(1),
        name="mlp_final" if final else "mlp",
    )(*args)


def _rope_tables(n_tokens, rot_dim):
    t = jnp.arange(n_tokens, dtype=jnp.int32)
    row = (t // GRID_W).astype(F32)
    col = (t % GRID_W).astype(F32)
    half = rot_dim // 2
    inv_freq = ROPE_THETA ** (-jnp.arange(0, half, 2, dtype=F32) / half)
    ar, ac = row[:, None] * inv_freq, col[:, None] * inv_freq
    cr, sr, cc, sn = jnp.cos(ar), jnp.sin(ar), jnp.cos(ac), jnp.sin(ac)
    cos = jnp.concatenate([cr, cr, cc, cc], axis=-1).T
    sin = jnp.concatenate([-sr, sr, -sn, sn], axis=-1).T
    cos = jnp.concatenate([cos, jnp.ones((rot_dim, TM), F32)], axis=1)
    sin = jnp.concatenate([sin, jnp.zeros((rot_dim, TM), F32)], axis=1)
    return cos, sin


def kernel(x, c, ctx, c_ctx, w_mod, b_mod, norm1_g, norm2_g, w_in, gqa_q_norm, gqa_k_norm, na_rpb,
           mla_kv_norm, mla_w_uk, mla_w_uv, w_o_gqa, w_o_na, w_o_mla, w_out, w_mlp1, w_mlp2,
           final_norm_g):
    batch, seq, d = x.shape
    n_ctx = ctx.shape[1]
    depth = w_mod.shape[0]
    assert d == D_MODEL and seq % (NA_QROWS * GRID_W) == 0 and n_ctx == TM
    rows = seq // GRID_W
    assert rows >= NA_KBLK * NA_QROWS
    tps = seq // TM
    tq = min(TQ, seq)

    cond = jnp.concatenate([c, c_ctx[None, :]], axis=0)
    cond = jnp.pad(cond, ((0, -(batch + 1) % 8), (0, 0)))
    mod_all = _modulation(cond, w_mod, b_mod)

    rope = _rope_tables(seq, HEAD_DIM) + _rope_tables(seq, MLA_ROPE)

    xt = x.reshape(batch * seq, d)
    ct = ctx.reshape(batch * n_ctx, d)
    group = GQA_HEADS // GQA_KV_HEADS

    def lat_row(i):
        return i // tps

    def ctx_row(i):
        return batch

    for l in range(depth):
        with_ctx = l < depth - 1
        mod = mod_all[l, :batch + 1].reshape(batch + 1, 1, 6 * d)
        w_t = w_in[l][:, :O_GATE].T.astype(BF16)
        w_g = w_in[l][:, O_GATE:].astype(BF16)
        g1 = norm1_g[l].reshape(1, d)
        g2 = norm2_g[l].reshape(1, d)
        qa, ka, va, qn, kn, vn, qm, km, vm = _project(
            xt, ct, mod, batch, g1, w_t, gqa_q_norm[l].reshape(-1, 1), gqa_k_norm[l].reshape(-1, 1),
            mla_kv_norm[l].reshape(-1, 1), mla_w_uk[l].T.astype(BF16), mla_w_uv[l].T.astype(BF16), rope)
        bias = _na_bias(na_rpb[l], rows)

        att_a = functools.partial(_attention, qa, ka, va, batch, tps, lambda h: 0, lambda h: h // group)
        att_n = functools.partial(_attention, qn, kn, vn, batch, tps, lambda h: h // 2, lambda h: h)
        att_m = functools.partial(_attention, qm, km, vm, batch, tps, lambda h: h, lambda h: h)
        ya = att_a(False, tq)
        yb = _na_attention(qn, kn, vn, bias, batch, tps)
        yc = att_m(False, tq)

        mw = (w_g, w_o_gqa[l].astype(BF16), w_o_na[l].astype(BF16), w_o_mla[l].astype(BF16),
              w_out[l].astype(BF16))
        w1 = w_mlp1[l].astype(BF16)
        w2 = w_mlp2[l].astype(BF16)
        gf = None if with_ctx else final_norm_g.reshape(1, d)
        x1 = _merge(xt, mod, lat_row, g1, ya, yb, yc, *mw)
        xt = _mlp(x1, mod, lat_row, g2, w1, w2, gf)

        if with_ctx:
            c1 = _merge(ct, mod, ctx_row, g1, att_a(True), att_n(True, q_chunked=True), att_m(True), *mw)
            ct = _mlp(c1, mod, ctx_row, g2, w1, w2, None)

    return xt.reshape(batch, seq, d)
```

```python
import functools
import math

import numpy as np
import jax
import jax.numpy as jnp
from jax import lax
from jax.experimental import pallas as pl
from jax.experimental.pallas import tpu as pltpu

F32 = jnp.float32
BF16 = jnp.bfloat16

D_MODEL = 1024
GRID_W = 64
HEAD_DIM = 64
GQA_HEADS = 8
GQA_KV_HEADS = 2
NA_HEADS = 8
NA_KH = 8
NA_KW = 16
MLA_HEADS = 8
MLA_NOPE = 64
MLA_ROPE = 32
MLA_QK = MLA_NOPE + MLA_ROPE
MLA_KV_RANK = 256
D_FF = 4 * D_MODEL
ROPE_THETA = 10000.0
EPS = 1e-6
LOG2E = math.log2(math.e)

O_QA = 0
O_KA = O_QA + GQA_HEADS * HEAD_DIM
O_VA = O_KA + GQA_KV_HEADS * HEAD_DIM
O_QN = O_VA + GQA_KV_HEADS * HEAD_DIM
O_KN = O_QN + NA_HEADS * HEAD_DIM
O_VN = O_KN + NA_HEADS * HEAD_DIM
O_QM = O_VN + NA_HEADS * HEAD_DIM
O_CKV = O_QM + MLA_HEADS * MLA_QK
O_KR = O_CKV + MLA_KV_RANK
O_GATE = O_KR + MLA_ROPE

TM = 256
TQ = 1024
KPAD = 128
ONES_ROWS = 16
NA_QROWS = 4
NA_UNROLL = 10
NA_KBLK = 3
NEG = -1e30
VMEM_LIMIT = 56 * 1024 * 1024

QSCALE_A = HEAD_DIM ** -0.5 * LOG2E
QSCALE_M = MLA_QK ** -0.5 * LOG2E


def _cparams(n_axes):
    return pltpu.CompilerParams(dimension_semantics=("parallel",) * n_axes,
                                vmem_limit_bytes=VMEM_LIMIT)


def _const_spec(shape):
    nd = len(shape)
    return pl.BlockSpec(shape, lambda *_: (0,) * nd, pipeline_mode=pl.Buffered(1))


def _rms_mod(x, g, scale, shift):
    ms = jnp.mean(x * x, axis=-1, keepdims=True)
    return (x * lax.rsqrt(ms + EPS) * g) * (1.0 + scale) + shift


def _rms_fm(x, g):
    ms = jnp.mean(x * x, axis=0, keepdims=True)
    return x * lax.rsqrt(ms + EPS) * g


def _rope_fm(x, cos, sin, blk):
    sw = jnp.concatenate([x[blk:2 * blk], x[0:blk], x[3 * blk:4 * blk], x[2 * blk:3 * blk]], axis=0)
    return x * cos + sw * sin


def _mod_kernel(c_ref, w_ref, b_ref, o_ref):
    c = c_ref[...]
    c = (c * jax.nn.sigmoid(c)).astype(BF16)
    w = w_ref[0].astype(BF16)
    o_ref[0] = jnp.dot(c, w, preferred_element_type=F32) + b_ref[0]


def _modulation(cond, w_mod, b_mod):
    depth, d, n = w_mod.shape
    rows = cond.shape[0]
    tn = 1536
    return pl.pallas_call(
        _mod_kernel,
        grid=(depth, n // tn),
        in_specs=[pl.BlockSpec((rows, d), lambda l, j: (0, 0)),
                  pl.BlockSpec((1, d, tn), lambda l, j: (l, 0, j)),
                  pl.BlockSpec((1, 1, tn), lambda l, j: (l, 0, j))],
        out_specs=pl.BlockSpec((1, rows, tn), lambda l, j: (l, 0, j)),
        out_shape=jax.ShapeDtypeStruct((depth, rows, n), F32),
        compiler_params=_cparams(2),
        name="modulation",
    )(cond, w_mod, b_mod.reshape(depth, 1, n))


def _proj_kernel(x_ref, c_ref, mod_ref, g_ref, w_ref, gq_ref, gk_ref, gkv_ref, wuk_ref, wuv_ref,
                 cosa_ref, sina_ref, cosm_ref, sinm_ref,
                 qa_o, ka_o, va_o, qn_o, kn_o, vn_o, qm_o, km_o, vm_o, *, n_lat):
    d = D_MODEL
    x = jnp.where(pl.program_id(0) < n_lat, x_ref[...], c_ref[...])
    m = mod_ref[0]
    h = _rms_mod(x, g_ref[...], m[:, d:2 * d], m[:, 0:d]).astype(BF16)
    tm = x.shape[0]

    def seg(a, b):
        return lax.dot_general(w_ref[a:b, :], h, (((1,), (1,)), ((), ())),
                               preferred_element_type=F32)

    def rope_a(t):
        return _rope_fm(t, cosa_ref[...], sina_ref[...], HEAD_DIM // 4)

    def rope_m(t):
        return _rope_fm(t, cosm_ref[...], sinm_ref[...], MLA_ROPE // 4)

    zeros64 = jnp.zeros((HEAD_DIM, tm), BF16)

    p = seg(O_QA, O_KA)
    group = GQA_HEADS // GQA_KV_HEADS
    for hh in range(GQA_HEADS):
        q = rope_a(_rms_fm(p[hh * HEAD_DIM:(hh + 1) * HEAD_DIM], gq_ref[...])) * QSCALE_A
        kvh = hh // group
        for j in range(KPAD // HEAD_DIM):
            r0 = hh * KPAD + j * HEAD_DIM
            qa_o[0, r0:r0 + HEAD_DIM, :] = q.astype(BF16) if j == kvh else zeros64

    p = seg(O_KA, O_VA)
    ks = [rope_a(_rms_fm(p[g * HEAD_DIM:(g + 1) * HEAD_DIM], gk_ref[...])) for g in range(GQA_KV_HEADS)]
    ka_o[...] = jnp.concatenate(ks, axis=0).T.astype(BF16)
    va_o[0] = seg(O_VA, O_QN).astype(BF16)

    p = seg(O_QN, O_KN)
    for hh in range(NA_HEADS):
        q = (p[hh * HEAD_DIM:(hh + 1) * HEAD_DIM] * QSCALE_A).astype(BF16)
        for j in range(KPAD // HEAD_DIM):
            r0 = hh * KPAD + j * HEAD_DIM
            qn_o[0, r0:r0 + HEAD_DIM, :] = q if j == hh % 2 else zeros64
    kn_o[...] = seg(O_KN, O_VN).T.astype(BF16)
    vn_o[0] = seg(O_VN, O_QM).astype(BF16)

    p = seg(O_QM, O_CKV)
    zeros_pad = jnp.zeros((KPAD - MLA_QK, tm), BF16)
    for hh in range(MLA_HEADS):
        base = hh * MLA_QK
        r0 = hh * KPAD
        qm_o[0, r0:r0 + MLA_NOPE, :] = (p[base:base + MLA_NOPE] * QSCALE_M).astype(BF16)
        qm_o[0, r0 + MLA_NOPE:r0 + MLA_QK, :] = (rope_m(p[base + MLA_NOPE:base + MLA_QK]) * QSCALE_M).astype(BF16)
        qm_o[0, r0 + MLA_QK:r0 + KPAD, :] = zeros_pad

    ckv = _rms_fm(seg(O_CKV, O_KR), gkv_ref[...]).astype(BF16)
    k_nope = jnp.dot(wuk_ref[...], ckv, preferred_element_type=F32)
    vm_o[0] = jnp.dot(wuv_ref[...], ckv, preferred_element_type=F32).astype(BF16)
    k_rope = rope_m(seg(O_KR, O_GATE))
    zpad = jnp.zeros((KPAD - MLA_QK, tm), F32)
    blocks = []
    for hh in range(MLA_HEADS):
        blocks += [k_nope[hh * MLA_NOPE:(hh + 1) * MLA_NOPE], k_rope, zpad]
    km_o[...] = jnp.concatenate(blocks, axis=0).T.astype(BF16)


def _project(xt, ct, mod, batch, g1, w_t, gq, gk, gkv, wuk_t, wuv_t, rope):
    d = xt.shape[1]
    n_lat = xt.shape[0] // TM
    n_ctx = ct.shape[0] // TM
    assert n_ctx == batch
    tps = n_lat // batch
    nt = n_lat + n_ctx
    t_all = nt * TM

    def is_lat(i):
        return i < n_lat

    def kv_blk(i):
        return jnp.where(is_lat(i), (i // tps) * (tps + 1) + i % tps, (i - n_lat) * (tps + 1) + tps)

    def rope_spec(tab):
        return pl.BlockSpec((tab.shape[0], TM), lambda i: (0, jnp.where(is_lat(i), i % tps, tps)))

    in_specs = [
        pl.BlockSpec((TM, d), lambda i: (jnp.minimum(i, n_lat - 1), 0)),
        pl.BlockSpec((TM, d), lambda i: (jnp.maximum(i - n_lat, 0), 0)),
        pl.BlockSpec((1, 1, 6 * d), lambda i: (jnp.where(is_lat(i), i // tps, batch), 0, 0)),
        _const_spec((1, d)),
        _const_spec(w_t.shape),
        _const_spec((HEAD_DIM, 1)),
        _const_spec((HEAD_DIM, 1)),
        _const_spec((MLA_KV_RANK, 1)),
        _const_spec(wuk_t.shape),
        _const_spec(wuv_t.shape),
    ] + [rope_spec(tab) for tab in rope]
    nh = NA_HEADS * HEAD_DIM
    ga = GQA_KV_HEADS * HEAD_DIM
    out_shape = [
        jax.ShapeDtypeStruct((nt, GQA_HEADS * KPAD, TM), BF16),
        jax.ShapeDtypeStruct((t_all, KPAD), BF16),
        jax.ShapeDtypeStruct((nt, ga, TM), BF16),
        jax.ShapeDtypeStruct((nt, NA_HEADS * KPAD, TM), BF16),
        jax.ShapeDtypeStruct((t_all, nh), BF16),
        jax.ShapeDtypeStruct((nt, nh, TM), BF16),
        jax.ShapeDtypeStruct((nt, MLA_HEADS * KPAD, TM), BF16),
        jax.ShapeDtypeStruct((t_all, MLA_HEADS * KPAD), BF16),
        jax.ShapeDtypeStruct((nt, MLA_HEADS * HEAD_DIM, TM), BF16),
    ]
    out_specs = [
        pl.BlockSpec((1, GQA_HEADS * KPAD, TM), lambda i: (i, 0, 0)),
        pl.BlockSpec((TM, KPAD), lambda i: (kv_blk(i), 0)),
        pl.BlockSpec((1, ga, TM), lambda i: (kv_blk(i), 0, 0)),
        pl.BlockSpec((1, NA_HEADS * KPAD, TM), lambda i: (i, 0, 0)),
        pl.BlockSpec((TM, nh), lambda i: (kv_blk(i), 0)),
        pl.BlockSpec((1, nh, TM), lambda i: (kv_blk(i), 0, 0)),
        pl.BlockSpec((1, MLA_HEADS * KPAD, TM), lambda i: (i, 0, 0)),
        pl.BlockSpec((TM, MLA_HEADS * KPAD), lambda i: (kv_blk(i), 0)),
        pl.BlockSpec((1, MLA_HEADS * HEAD_DIM, TM), lambda i: (kv_blk(i), 0, 0)),
    ]
    return pl.pallas_call(
        functools.partial(_proj_kernel, n_lat=n_lat),
        grid=(nt,),
        in_specs=in_specs,
        out_specs=out_specs,
        out_shape=out_shape,
        compiler_params=_cparams(1),
        name="project",
    )(xt, ct, mod, g1, w_t, gq, gk, gkv, wuk_t, wuv_t, *rope)


def _attn_kernel(q_ref, k_ref, v_ref, o_ref, s_buf, *, n_chunks, nq, tiles):
    tq = tiles * TM
    dv = o_ref.shape[1]
    ones = jnp.ones((ONES_ROWS, TM), BF16)
    neg = jnp.full((1, tq), NEG, F32)
    first_slot = 2

    def slot_of(c):
        return first_slot if c == 0 else c % 2

    def load_q(i):
        return jnp.concatenate([q_ref[i * tiles + j] for j in range(tiles)], axis=1)

    def produce(q, c, m_prev):
        s = jnp.dot(k_ref[c * TM:(c + 1) * TM, :], q, preferred_element_type=F32)
        s_buf[slot_of(c)] = s
        m_new = jnp.maximum(m_prev, jnp.max(s, axis=0, keepdims=True))
        return m_new, jnp.exp2(m_prev - m_new)

    def consume(c, m, alpha, acc):
        p = jnp.exp2(s_buf[slot_of(c)] - m).astype(BF16)
        v_ext = jnp.concatenate([v_ref[c], ones], axis=0)
        pv = jnp.dot(v_ext, p, preferred_element_type=F32)
        return pv if acc is None else alpha * acc + pv

    def block(i, carry):
        m, alpha = carry
        q = load_q(i)
        acc = None
        for c in range(n_chunks):
            if c + 1 < n_chunks:
                m_next, a_next = produce(q, c + 1, m)
            elif nq > 1:
                m_next, a_next = produce(load_q(jnp.minimum(i + 1, nq - 1)), 0, neg)
            else:
                m_next, a_next = m, alpha
            acc = consume(c, m, alpha, acc)
            m, alpha = m_next, a_next
        out = (acc[0:dv] / acc[dv:dv + 1]).astype(o_ref.dtype)
        for j in range(tiles):
            o_ref[i * tiles + j] = out[:, j * TM:(j + 1) * TM]
        return m, alpha

    carry = produce(load_q(0), 0, neg)
    if nq > 1:
        lax.fori_loop(0, nq, block, carry)
    else:
        block(0, carry)


def _attention(q, k, v, batch, tps, kcol, vrow, ctx_queries, tq=TM):
    heads = q.shape[1] // KPAD
    per_b = tps + 1
    if ctx_queries:
        q_tiles, tq, n_chunks = 1, TM, 1
        q_spec = pl.BlockSpec((1, KPAD, TM), lambda b, h: (batch * tps + b, h, 0))
        k_spec = pl.BlockSpec((TM, KPAD), lambda b, h: (b * per_b + tps, kcol(h)))
        v_spec = pl.BlockSpec((1, HEAD_DIM, TM), lambda b, h: (b * per_b + tps, vrow(h), 0))
    else:
        q_tiles, n_chunks = tps, per_b
        q_spec = pl.BlockSpec((tps, KPAD, TM), lambda b, h: (b, h, 0))
        k_spec = pl.BlockSpec((per_b * TM, KPAD), lambda b, h: (b, kcol(h)))
        v_spec = pl.BlockSpec((per_b, HEAD_DIM, TM), lambda b, h: (b, vrow(h), 0))
    tiles = tq // TM
    return pl.pallas_call(
        functools.partial(_attn_kernel, n_chunks=n_chunks, nq=q_tiles // tiles, tiles=tiles),
        grid=(batch, heads),
        in_specs=[q_spec, k_spec, v_spec],
        out_specs=pl.BlockSpec((q_tiles, HEAD_DIM, TM), lambda b, h: (b, h, 0)),
        out_shape=jax.ShapeDtypeStruct((batch * q_tiles, heads * HEAD_DIM, TM), BF16),
        scratch_shapes=[pltpu.VMEM((3, TM, tq), F32)],
        compiler_params=_cparams(2),
        name="ctx_attention" if ctx_queries else "attention",
    )(q, k, v)


def _na_kernel(q_ref, k_ref, v_ref, bias_ref, o_ref, s_buf, *, nq):
    win = NA_KBLK * TM
    ones = jnp.ones((ONES_ROWS, TM), BF16)
    static = lambda i: isinstance(i, int)

    def kblk(i):
        return min(max(i - 1, 0), nq - NA_KBLK) if static(i) else jnp.clip(i - 1, 0, nq - NA_KBLK)

    def produce(i, slot):
        if static(i):
            case, off = (0 if i == 0 else 2 if i == nq - 1 else 1), kblk(i) * TM
        else:
            case = jnp.where(i == 0, 0, jnp.where(i == nq - 1, 2, 1))
            off = pl.multiple_of(kblk(i) * TM, TM)
        q = q_ref[i]
        s_win = jnp.dot(k_ref[pl.ds(off, win), :], q, preferred_element_type=F32) + bias_ref[0, case]
        s_ctx = jnp.dot(k_ref[pl.ds(nq * TM, TM), :], q, preferred_element_type=F32)
        s_buf[slot, 0:win, :] = s_win
        s_buf[slot, win:win + TM, :] = s_ctx
        return jnp.maximum(jnp.max(s_win, axis=0, keepdims=True), jnp.max(s_ctx, axis=0, keepdims=True))

    def consume(i, slot, m):
        p = jnp.exp2(s_buf[slot] - m).astype(BF16)
        kb = kblk(i)
        chunks = [v_ref[kb + j] for j in range(NA_KBLK)] + [v_ref[nq]]
        v_all = jnp.concatenate([jnp.concatenate([c, ones], axis=0) for c in chunks], axis=1)
        acc = jnp.dot(v_all, p, preferred_element_type=F32)
        dv = o_ref.shape[1]
        o_ref[i] = (acc[0:dv] / acc[dv:dv + 1]).astype(o_ref.dtype)

    unroll = max(u for u in range(2, NA_UNROLL + 1, 2) if (nq - 2) % u == 0)
    m = produce(0, 0)

    def body(it, m):
        c = unroll * it
        for j in range(unroll):
            m_next = produce(c + j + 1, (j + 1) % 2)
            consume(c + j, j % 2, m)
            m = m_next
        return m

    m = lax.fori_loop(0, (nq - 2) // unroll, body, m)
    m_last = produce(nq - 1, (nq - 1) % 2)
    consume(nq - 2, (nq - 2) % 2, m)
    consume(nq - 1, (nq - 1) % 2, m_last)


def _na_bias(rpb, rows):
    heads = rpb.shape[0]
    kc = np.arange(GRID_W)[:, None]
    c = np.arange(GRID_W)[None, :]
    cs = np.clip(c - NA_KW // 2, 0, GRID_W - NA_KW)
    valid_c = (kc >= cs) & (kc < cs + NA_KW)
    ci = kc - c + (NA_KW - 1)
    rpb = rpb.astype(F32) * LOG2E
    toe = jnp.full((heads, 2 * NA_KH - 1, GRID_W, GRID_W), NEG, F32)
    for j in range(2 * NA_KW - 1):
        toe = jnp.where(jnp.asarray(valid_c & (ci == j)), rpb[:, :, j][:, :, None, None], toe)
    n_dr = 2 * NA_KH - 1
    toe = jnp.concatenate([toe, jnp.full((heads, 1, GRID_W, GRID_W), NEG, F32)], axis=1)
    n_kr = NA_KBLK * NA_QROWS
    dr_idx = np.full((3, n_kr, NA_QROWS), n_dr, np.int32)
    for case, r0 in enumerate((0, NA_QROWS, rows - NA_QROWS)):
        ks = min(max(r0 - NA_KH // 2, 0), rows - n_kr)
        for a in range(n_kr):
            for b in range(NA_QROWS):
                kr, r = ks + a, r0 + b
                rs = min(max(r - NA_KH // 2, 0), rows - NA_KH)
                if rs <= kr < rs + NA_KH:
                    dr_idx[case, a, b] = kr - r + NA_KH - 1
    t = jnp.take(toe, jnp.asarray(dr_idx.reshape(-1)), axis=1)
    t = t.reshape(heads, 3, n_kr, NA_QROWS, GRID_W, GRID_W)
    t = jnp.transpose(t, (0, 1, 2, 4, 3, 5))
    return t.reshape(heads, 3, n_kr * GRID_W, NA_QROWS * GRID_W)


def _na_attention(q, k, v, bias, batch, tps):
    heads = q.shape[1] // KPAD
    nq = tps
    per_b = tps + 1
    assert nq >= NA_KBLK and nq % 2 == 0
    return pl.pallas_call(
        functools.partial(_na_kernel, nq=nq),
        grid=(batch, heads),
        in_specs=[pl.BlockSpec((nq, KPAD, TM), lambda b, h: (b, h, 0)),
                  pl.BlockSpec((per_b * TM, KPAD), lambda b, h: (b, h // 2)),
                  pl.BlockSpec((per_b, HEAD_DIM, TM), lambda b, h: (b, h, 0)),
                  pl.BlockSpec((1, 3, NA_KBLK * TM, TM), lambda b, h: (h, 0, 0, 0))],
        out_specs=pl.BlockSpec((nq, HEAD_DIM, TM), lambda b, h: (b, h, 0)),
        out_shape=jax.ShapeDtypeStruct((batch * nq, heads * HEAD_DIM, TM), BF16),
        scratch_shapes=[pltpu.VMEM((2, (NA_KBLK + 1) * TM, TM), F32)],
        compiler_params=_cparams(2),
        name="neighbourhood_attention",
    )(q, k, v, bias)


def _merge_mlp_kernel(*refs, final):
    (x_ref, mod_ref, g1_ref, g2_ref, ya_ref, yb_ref, yc_ref, wg_ref, woa_ref, wob_ref, woc_ref,
     wout_ref, w1_ref, w2_ref) = refs[:14]
    o_ref = refs[-1]
    d = D_MODEL
    x = x_ref[...]
    m = mod_ref[0]
    h = _rms_mod(x, g1_ref[...], m[:, d:2 * d], m[:, 0:d]).astype(BF16)
    y = None
    for i, (y_ref, wo_ref) in enumerate(((ya_ref, woa_ref), (yb_ref, wob_ref), (yc_ref, woc_ref))):
        gate = jax.nn.sigmoid(jnp.dot(h, wg_ref[:, i * d:(i + 1) * d], preferred_element_type=F32))
        yt = y_ref[0].astype(F32).T.astype(BF16)
        u = gate * jnp.dot(yt, wo_ref[...], preferred_element_type=F32)
        y = u if y is None else y + u
    a = jnp.dot(y.astype(BF16), wout_ref[...], preferred_element_type=F32)
    x = x + m[:, 2 * d:3 * d] * a

    h = _rms_mod(x, g2_ref[...], m[:, 4 * d:5 * d], m[:, 3 * d:4 * d]).astype(BF16)
    a = None
    for c in range(D_FF // d):
        u = jnp.dot(h, w1_ref[:, c * d:(c + 1) * d], preferred_element_type=F32)
        u = jnp.square(jnp.maximum(u, 0.0)).astype(BF16)
        t = jnp.dot(u, w2_ref[c * d:(c + 1) * d, :], preferred_element_type=F32)
        a = t if a is None else a + t
    out = x + m[:, 5 * d:6 * d] * a
    if final:
        gf = refs[14][...]
        ms = jnp.mean(out * out, axis=-1, keepdims=True)
        out = out * lax.rsqrt(ms + EPS) * gf
    o_ref[...] = out


def _merge_mlp(tok, mod, mod_row, g1, g2, ya, yb, yc, weights, gf):
    t, d = tok.shape
    final = gf is not None
    y_spec = pl.BlockSpec((1, ya.shape[1], TM), lambda i: (i, 0, 0))
    in_specs = [pl.BlockSpec((TM, d), lambda i: (i, 0)),
                pl.BlockSpec((1, 1, 6 * d), lambda i: (mod_row(i), 0, 0)),
                _const_spec((1, d)), _const_spec((1, d)),
                y_spec, y_spec, y_spec] + [_const_spec(w.shape) for w in weights]
    args = [tok, mod, g1, g2, ya, yb, yc, *weights]
    if final:
        in_specs.append(_const_spec((1, d)))
        args.append(gf)
    return pl.pallas_call(
        functools.partial(_merge_mlp_kernel, final=final),
        grid=(t // TM,),
        in_specs=in_specs,
        out_specs=pl.BlockSpec((TM, d), lambda i: (i, 0)),
        out_shape=jax.ShapeDtypeStruct((t, d), F32),
        compiler_params=_cparams(1),
        name="merge_mlp_final" if final else "merge_mlp",
    )(*args)


def _rope_tables(n_tokens, rot_dim):
    t = jnp.arange(n_tokens, dtype=jnp.int32)
    row = (t // GRID_W).astype(F32)
    col = (t % GRID_W).astype(F32)
    half = rot_dim // 2
    inv_freq = ROPE_THETA ** (-jnp.arange(0, half, 2, dtype=F32) / half)
    ar, ac = row[:, None] * inv_freq, col[:, None] * inv_freq
    cr, sr, cc, sn = jnp.cos(ar), jnp.sin(ar), jnp.cos(ac), jnp.sin(ac)
    cos = jnp.concatenate([cr, cr, cc, cc], axis=-1).T
    sin = jnp.concatenate([-sr, sr, -sn, sn], axis=-1).T
    cos = jnp.concatenate([cos, jnp.ones((rot_dim, TM), F32)], axis=1)
    sin = jnp.concatenate([sin, jnp.zeros((rot_dim, TM), F32)], axis=1)
    return cos, sin


def kernel(x, c, ctx, c_ctx, w_mod, b_mod, norm1_g, norm2_g, w_in, gqa_q_norm, gqa_k_norm, na_rpb,
           mla_kv_norm, mla_w_uk, mla_w_uv, w_o_gqa, w_o_na, w_o_mla, w_out, w_mlp1, w_mlp2,
           final_norm_g):
    batch, seq, d = x.shape
    n_ctx = ctx.shape[1]
    depth = w_mod.shape[0]
    assert d == D_MODEL and seq % (NA_QROWS * GRID_W) == 0 and n_ctx == TM
    rows = seq // GRID_W
    assert rows >= NA_KBLK * NA_QROWS
    tps = seq // TM
    tq = min(TQ, seq)

    cond = jnp.concatenate([c, c_ctx[None, :]], axis=0)
    cond = jnp.pad(cond, ((0, -(batch + 1) % 8), (0, 0)))
    mod_all = _modulation(cond, w_mod, b_mod)

    rope = _rope_tables(seq, HEAD_DIM) + _rope_tables(seq, MLA_ROPE)

    xt = x.reshape(batch * seq, d)
    ct = ctx.reshape(batch * n_ctx, d)
    group = GQA_HEADS // GQA_KV_HEADS

    def lat_row(i):
        return i // tps

    def ctx_row(i):
        return batch

    for l in range(depth):
        with_ctx = l < depth - 1
        mod = mod_all[l, :batch + 1].reshape(batch + 1, 1, 6 * d)
        w_t = w_in[l][:, :O_GATE].T.astype(BF16)
        w_g = w_in[l][:, O_GATE:].astype(BF16)
        g1 = norm1_g[l].reshape(1, d)
        g2 = norm2_g[l].reshape(1, d)
        qa, ka, va, qn, kn, vn, qm, km, vm = _project(
            xt, ct, mod, batch, g1, w_t, gqa_q_norm[l].reshape(-1, 1), gqa_k_norm[l].reshape(-1, 1),
            mla_kv_norm[l].reshape(-1, 1), mla_w_uk[l].T.astype(BF16), mla_w_uv[l].T.astype(BF16), rope)
        bias = _na_bias(na_rpb[l], rows)

        att_a = functools.partial(_attention, qa, ka, va, batch, tps, lambda h: 0, lambda h: h // group)
        att_n = functools.partial(_attention, qn, kn, vn, batch, tps, lambda h: h // 2, lambda h: h)
        att_m = functools.partial(_attention, qm, km, vm, batch, tps, lambda h: h, lambda h: h)
        ya = att_a(False, tq)
        yb = _na_attention(qn, kn, vn, bias, batch, tps)
        yc = att_m(False, tq)

        weights = (w_g, w_o_gqa[l].astype(BF16), w_o_na[l].astype(BF16), w_o_mla[l].astype(BF16),
                   w_out[l].astype(BF16), w_mlp1[l].astype(BF16), w_mlp2[l].astype(BF16))
        gf = None if with_ctx else final_norm_g.reshape(1, d)
        xt = _merge_mlp(xt, mod, lat_row, g1, g2, ya, yb, yc, weights, gf)
        if with_ctx:
            ct = _merge_mlp(ct, mod, ctx_row, g1, g2, att_a(True), att_n(True), att_m(True), weights, None)

    return xt.reshape(batch, seq, d)
```

```python
import functools
import math

import numpy as np
import jax
import jax.numpy as jnp
from jax import lax
from jax.experimental import pallas as pl
from jax.experimental.pallas import tpu as pltpu

F32 = jnp.float32
BF16 = jnp.bfloat16

D_MODEL = 1024
GRID_W = 64
HEAD_DIM = 64
GQA_HEADS = 8
GQA_KV_HEADS = 2
NA_HEADS = 8
NA_KH = 8
NA_KW = 16
MLA_HEADS = 8
MLA_NOPE = 64
MLA_ROPE = 32
MLA_QK = MLA_NOPE + MLA_ROPE
MLA_KV_RANK = 256
D_FF = 4 * D_MODEL
ROPE_THETA = 10000.0
EPS = 1e-6
LOG2E = math.log2(math.e)

O_QA = 0
O_KA = O_QA + GQA_HEADS * HEAD_DIM
O_VA = O_KA + GQA_KV_HEADS * HEAD_DIM
O_QN = O_VA + GQA_KV_HEADS * HEAD_DIM
O_KN = O_QN + NA_HEADS * HEAD_DIM
O_VN = O_KN + NA_HEADS * HEAD_DIM
O_QM = O_VN + NA_HEADS * HEAD_DIM
O_CKV = O_QM + MLA_HEADS * MLA_QK
O_KR = O_CKV + MLA_KV_RANK
O_GATE = O_KR + MLA_ROPE

TM = 256
TQ = 1024
ATTN_KC = 1
KPAD = 128
ONES_ROWS = 16
NA_QROWS = 4
NA_UNROLL = 10
NA_KBLK = 3
NEG = -1e30
VMEM_LIMIT = 56 * 1024 * 1024

QSCALE_A = HEAD_DIM ** -0.5 * LOG2E
QSCALE_M = MLA_QK ** -0.5 * LOG2E


def _cparams(n_axes):
    return pltpu.CompilerParams(dimension_semantics=("parallel",) * n_axes,
                                vmem_limit_bytes=VMEM_LIMIT)


def _const_spec(shape):
    nd = len(shape)
    return pl.BlockSpec(shape, lambda *_: (0,) * nd, pipeline_mode=pl.Buffered(1))


def _rms_mod(x, g, scale, shift):
    ms = jnp.mean(x * x, axis=-1, keepdims=True)
    return (x * lax.rsqrt(ms + EPS) * g) * (1.0 + scale) + shift


def _rms_fm(x, g):
    ms = jnp.mean(x * x, axis=0, keepdims=True)
    return x * lax.rsqrt(ms + EPS) * g


def _rope_fm(x, cos, sin, blk):
    sw = jnp.concatenate([x[blk:2 * blk], x[0:blk], x[3 * blk:4 * blk], x[2 * blk:3 * blk]], axis=0)
    return x * cos + sw * sin


def _mod_kernel(c_ref, w_ref, b_ref, o_ref):
    c = c_ref[...]
    c = (c * jax.nn.sigmoid(c)).astype(BF16)
    w = w_ref[0].astype(BF16)
    o_ref[0] = jnp.dot(c, w, preferred_element_type=F32) + b_ref[0]


def _modulation(cond, w_mod, b_mod):
    depth, d, n = w_mod.shape
    rows = cond.shape[0]
    tn = 1536
    return pl.pallas_call(
        _mod_kernel,
        grid=(depth, n // tn),
        in_specs=[pl.BlockSpec((rows, d), lambda l, j: (0, 0)),
                  pl.BlockSpec((1, d, tn), lambda l, j: (l, 0, j)),
                  pl.BlockSpec((1, 1, tn), lambda l, j: (l, 0, j))],
        out_specs=pl.BlockSpec((1, rows, tn), lambda l, j: (l, 0, j)),
        out_shape=jax.ShapeDtypeStruct((depth, rows, n), F32),
        compiler_params=_cparams(2),
        name="modulation",
    )(cond, w_mod, b_mod.reshape(depth, 1, n))


def _proj_kernel(x_ref, c_ref, mod_ref, g_ref, w_ref, gq_ref, gk_ref, gkv_ref, wuk_ref, wuv_ref,
                 cosa_ref, sina_ref, cosm_ref, sinm_ref,
                 qa_o, ka_o, va_o, qn_o, kn_o, vn_o, qm_o, km_o, vm_o, *, n_lat):
    d = D_MODEL
    x = jnp.where(pl.program_id(0) < n_lat, x_ref[...], c_ref[...])
    m = mod_ref[0]
    h = _rms_mod(x, g_ref[...], m[:, d:2 * d], m[:, 0:d]).astype(BF16)
    tm = x.shape[0]

    def seg(a, b):
        return lax.dot_general(w_ref[a:b, :], h, (((1,), (1,)), ((), ())),
                               preferred_element_type=F32)

    def rope_a(t):
        return _rope_fm(t, cosa_ref[...], sina_ref[...], HEAD_DIM // 4)

    def rope_m(t):
        return _rope_fm(t, cosm_ref[...], sinm_ref[...], MLA_ROPE // 4)

    zeros64 = jnp.zeros((HEAD_DIM, tm), BF16)

    p = seg(O_QA, O_KA)
    group = GQA_HEADS // GQA_KV_HEADS
    for hh in range(GQA_HEADS):
        q = rope_a(_rms_fm(p[hh * HEAD_DIM:(hh + 1) * HEAD_DIM], gq_ref[...])) * QSCALE_A
        kvh = hh // group
        for j in range(KPAD // HEAD_DIM):
            r0 = hh * KPAD + j * HEAD_DIM
            qa_o[0, r0:r0 + HEAD_DIM, :] = q.astype(BF16) if j == kvh else zeros64

    p = seg(O_KA, O_VA)
    ks = [rope_a(_rms_fm(p[g * HEAD_DIM:(g + 1) * HEAD_DIM], gk_ref[...])) for g in range(GQA_KV_HEADS)]
    ka_o[...] = jnp.concatenate(ks, axis=0).T.astype(BF16)
    va_o[0] = seg(O_VA, O_QN).astype(BF16)

    p = seg(O_QN, O_KN)
    for hh in range(NA_HEADS):
        q = (p[hh * HEAD_DIM:(hh + 1) * HEAD_DIM] * QSCALE_A).astype(BF16)
        for j in range(KPAD // HEAD_DIM):
            r0 = hh * KPAD + j * HEAD_DIM
            qn_o[0, r0:r0 + HEAD_DIM, :] = q if j == hh % 2 else zeros64
    kn_o[...] = seg(O_KN, O_VN).T.astype(BF16)
    vn_o[0] = seg(O_VN, O_QM).astype(BF16)

    p = seg(O_QM, O_CKV)
    zeros_pad = jnp.zeros((KPAD - MLA_QK, tm), BF16)
    for hh in range(MLA_HEADS):
        base = hh * MLA_QK
        r0 = hh * KPAD
        qm_o[0, r0:r0 + MLA_NOPE, :] = (p[base:base + MLA_NOPE] * QSCALE_M).astype(BF16)
        qm_o[0, r0 + MLA_NOPE:r0 + MLA_QK, :] = (rope_m(p[base + MLA_NOPE:base + MLA_QK]) * QSCALE_M).astype(BF16)
        qm_o[0, r0 + MLA_QK:r0 + KPAD, :] = zeros_pad

    ckv = _rms_fm(seg(O_CKV, O_KR), gkv_ref[...]).astype(BF16)
    k_nope = jnp.dot(wuk_ref[...], ckv, preferred_element_type=F32)
    vm_o[0] = jnp.dot(wuv_ref[...], ckv, preferred_element_type=F32).astype(BF16)
    k_rope = rope_m(seg(O_KR, O_GATE))
    zpad = jnp.zeros((KPAD - MLA_QK, tm), F32)
    blocks = []
    for hh in range(MLA_HEADS):
        blocks += [k_nope[hh * MLA_NOPE:(hh + 1) * MLA_NOPE], k_rope, zpad]
    km_o[...] = jnp.concatenate(blocks, axis=0).T.astype(BF16)


def _project(xt, ct, mod, batch, g1, w_t, gq, gk, gkv, wuk_t, wuv_t, rope):
    d = xt.shape[1]
    n_lat = xt.shape[0] // TM
    n_ctx = ct.shape[0] // TM
    assert n_ctx == batch
    tps = n_lat // batch
    nt = n_lat + n_ctx
    t_all = nt * TM

    def is_lat(i):
        return i < n_lat

    def kv_blk(i):
        return jnp.where(is_lat(i), (i // tps) * (tps + 1) + i % tps, (i - n_lat) * (tps + 1) + tps)

    def rope_spec(tab):
        return pl.BlockSpec((tab.shape[0], TM), lambda i: (0, jnp.where(is_lat(i), i % tps, tps)))

    in_specs = [
        pl.BlockSpec((TM, d), lambda i: (jnp.minimum(i, n_lat - 1), 0)),
        pl.BlockSpec((TM, d), lambda i: (jnp.maximum(i - n_lat, 0), 0)),
        pl.BlockSpec((1, 1, 6 * d), lambda i: (jnp.where(is_lat(i), i // tps, batch), 0, 0)),
        _const_spec((1, d)),
        _const_spec(w_t.shape),
        _const_spec((HEAD_DIM, 1)),
        _const_spec((HEAD_DIM, 1)),
        _const_spec((MLA_KV_RANK, 1)),
        _const_spec(wuk_t.shape),
        _const_spec(wuv_t.shape),
    ] + [rope_spec(tab) for tab in rope]
    nh = NA_HEADS * HEAD_DIM
    ga = GQA_KV_HEADS * HEAD_DIM
    out_shape = [
        jax.ShapeDtypeStruct((nt, GQA_HEADS * KPAD, TM), BF16),
        jax.ShapeDtypeStruct((t_all, KPAD), BF16),
        jax.ShapeDtypeStruct((nt, ga, TM), BF16),
        jax.ShapeDtypeStruct((nt, NA_HEADS * KPAD, TM), BF16),
        jax.ShapeDtypeStruct((t_all, nh), BF16),
        jax.ShapeDtypeStruct((nt, nh, TM), BF16),
        jax.ShapeDtypeStruct((nt, MLA_HEADS * KPAD, TM), BF16),
        jax.ShapeDtypeStruct((t_all, MLA_HEADS * KPAD), BF16),
        jax.ShapeDtypeStruct((nt, MLA_HEADS * HEAD_DIM, TM), BF16),
    ]
    out_specs = [
        pl.BlockSpec((1, GQA_HEADS * KPAD, TM), lambda i: (i, 0, 0)),
        pl.BlockSpec((TM, KPAD), lambda i: (kv_blk(i), 0)),
        pl.BlockSpec((1, ga, TM), lambda i: (kv_blk(i), 0, 0)),
        pl.BlockSpec((1, NA_HEADS * KPAD, TM), lambda i: (i, 0, 0)),
        pl.BlockSpec((TM, nh), lambda i: (kv_blk(i), 0)),
        pl.BlockSpec((1, nh, TM), lambda i: (kv_blk(i), 0, 0)),
        pl.BlockSpec((1, MLA_HEADS * KPAD, TM), lambda i: (i, 0, 0)),
        pl.BlockSpec((TM, MLA_HEADS * KPAD), lambda i: (kv_blk(i), 0)),
        pl.BlockSpec((1, MLA_HEADS * HEAD_DIM, TM), lambda i: (kv_blk(i), 0, 0)),
    ]
    return pl.pallas_call(
        functools.partial(_proj_kernel, n_lat=n_lat),
        grid=(nt,),
        in_specs=in_specs,
        out_specs=out_specs,
        out_shape=out_shape,
        compiler_params=_cparams(1),
        name="project",
    )(xt, ct, mod, g1, w_t, gq, gk, gkv, wuk_t, wuv_t, *rope)


def _attn_kernel(q_ref, k_ref, v_ref, o_ref, s_buf, *, n_chunks, nq, tiles):
    tq = tiles * TM
    dv = o_ref.shape[1]
    ones = jnp.ones((ONES_ROWS, TM), BF16)
    neg = jnp.full((1, tq), NEG, F32)
    steps = [(c, min(ATTN_KC, n_chunks - c)) for c in range(0, n_chunks, ATTN_KC)]
    first_slot = 2

    def slot_of(t):
        return first_slot if t == 0 else t % 2

    def load_q(i):
        return jnp.concatenate([q_ref[i * tiles + j] for j in range(tiles)], axis=1)

    def produce(q, t, m_prev):
        c, n = steps[t]
        s = jnp.dot(k_ref[c * TM:(c + n) * TM, :], q, preferred_element_type=F32)
        s_buf[slot_of(t), 0:n * TM, :] = s
        m_new = jnp.maximum(m_prev, jnp.max(s, axis=0, keepdims=True))
        return m_new, jnp.exp2(m_prev - m_new)

    def consume(t, m, alpha, acc):
        c, n = steps[t]
        p = jnp.exp2(s_buf[slot_of(t), 0:n * TM, :] - m).astype(BF16)
        v_ext = jnp.concatenate([jnp.concatenate([v_ref[c + j], ones], axis=0) for j in range(n)], axis=1)
        pv = jnp.dot(v_ext, p, preferred_element_type=F32)
        return pv if acc is None else alpha * acc + pv

    def block(i, carry):
        m, alpha = carry
        q = load_q(i)
        acc = None
        for t in range(len(steps)):
            if t + 1 < len(steps):
                m_next, a_next = produce(q, t + 1, m)
            elif nq > 1:
                m_next, a_next = produce(load_q(jnp.minimum(i + 1, nq - 1)), 0, neg)
            else:
                m_next, a_next = m, alpha
            acc = consume(t, m, alpha, acc)
            m, alpha = m_next, a_next
        out = (acc[0:dv] / acc[dv:dv + 1]).astype(o_ref.dtype)
        for j in range(tiles):
            o_ref[i * tiles + j] = out[:, j * TM:(j + 1) * TM]
        return m, alpha

    carry = produce(load_q(0), 0, neg)
    if nq > 1:
        lax.fori_loop(0, nq, block, carry)
    else:
        block(0, carry)


def _attention(q, k, v, batch, tps, kcol, vrow, ctx_queries, tq=TM):
    heads = q.shape[1] // KPAD
    per_b = tps + 1
    if ctx_queries:
        q_tiles, tq, n_chunks = 1, TM, 1
        q_spec = pl.BlockSpec((1, KPAD, TM), lambda b, h: (batch * tps + b, h, 0))
        k_spec = pl.BlockSpec((TM, KPAD), lambda b, h: (b * per_b + tps, kcol(h)))
        v_spec = pl.BlockSpec((1, HEAD_DIM, TM), lambda b, h: (b * per_b + tps, vrow(h), 0))
    else:
        q_tiles, n_chunks = tps, per_b
        q_spec = pl.BlockSpec((tps, KPAD, TM), lambda b, h: (b, h, 0))
        k_spec = pl.BlockSpec((per_b * TM, KPAD), lambda b, h: (b, kcol(h)))
        v_spec = pl.BlockSpec((per_b, HEAD_DIM, TM), lambda b, h: (b, vrow(h), 0))
    tiles = tq // TM
    return pl.pallas_call(
        functools.partial(_attn_kernel, n_chunks=n_chunks, nq=q_tiles // tiles, tiles=tiles),
        grid=(batch, heads),
        in_specs=[q_spec, k_spec, v_spec],
        out_specs=pl.BlockSpec((q_tiles, HEAD_DIM, TM), lambda b, h: (b, h, 0)),
        out_shape=jax.ShapeDtypeStruct((batch * q_tiles, heads * HEAD_DIM, TM), BF16),
        scratch_shapes=[pltpu.VMEM((3, min(ATTN_KC, n_chunks) * TM, tq), F32)],
        compiler_params=_cparams(2),
        name="ctx_attention" if ctx_queries else "attention",
    )(q, k, v)


def _na_kernel(q_ref, k_ref, v_ref, bias_ref, o_ref, s_buf, *, nq):
    win = NA_KBLK * TM
    ones = jnp.ones((ONES_ROWS, TM), BF16)
    static = lambda i: isinstance(i, int)

    def kblk(i):
        return min(max(i - 1, 0), nq - NA_KBLK) if static(i) else jnp.clip(i - 1, 0, nq - NA_KBLK)

    def produce(i, slot):
        if static(i):
            case, off = (0 if i == 0 else 2 if i == nq - 1 else 1), kblk(i) * TM
        else:
            case = jnp.where(i == 0, 0, jnp.where(i == nq - 1, 2, 1))
            off = pl.multiple_of(kblk(i) * TM, TM)
        q = q_ref[i]
        s_win = jnp.dot(k_ref[pl.ds(off, win), :], q, preferred_element_type=F32) + bias_ref[0, case]
        s_ctx = jnp.dot(k_ref[pl.ds(nq * TM, TM), :], q, preferred_element_type=F32)
        s_buf[slot, 0:win, :] = s_win
        s_buf[slot, win:win + TM, :] = s_ctx
        return jnp.maximum(jnp.max(s_win, axis=0, keepdims=True), jnp.max(s_ctx, axis=0, keepdims=True))

    def consume(i, slot, m):
        p = jnp.exp2(s_buf[slot] - m).astype(BF16)
        kb = kblk(i)
        chunks = [v_ref[kb + j] for j in range(NA_KBLK)] + [v_ref[nq]]
        v_all = jnp.concatenate([jnp.concatenate([c, ones], axis=0) for c in chunks], axis=1)
        acc = jnp.dot(v_all, p, preferred_element_type=F32)
        dv = o_ref.shape[1]
        o_ref[i] = (acc[0:dv] / acc[dv:dv + 1]).astype(o_ref.dtype)

    unroll = max(u for u in range(2, NA_UNROLL + 1, 2) if (nq - 2) % u == 0)
    m = produce(0, 0)

    def body(it, m):
        c = unroll * it
        for j in range(unroll):
            m_next = produce(c + j + 1, (j + 1) % 2)
            consume(c + j, j % 2, m)
            m = m_next
        return m

    m = lax.fori_loop(0, (nq - 2) // unroll, body, m)
    m_last = produce(nq - 1, (nq - 1) % 2)
    consume(nq - 2, (nq - 2) % 2, m)
    consume(nq - 1, (nq - 1) % 2, m_last)


def _na_bias(rpb, rows):
    heads = rpb.shape[0]
    kc = np.arange(GRID_W)[:, None]
    c = np.arange(GRID_W)[None, :]
    cs = np.clip(c - NA_KW // 2, 0, GRID_W - NA_KW)
    valid_c = (kc >= cs) & (kc < cs + NA_KW)
    ci = kc - c + (NA_KW - 1)
    rpb = rpb.astype(F32) * LOG2E
    toe = jnp.full((heads, 2 * NA_KH - 1, GRID_W, GRID_W), NEG, F32)
    for j in range(2 * NA_KW - 1):
        toe = jnp.where(jnp.asarray(valid_c & (ci == j)), rpb[:, :, j][:, :, None, None], toe)
    n_dr = 2 * NA_KH - 1
    n_kr = NA_KBLK * NA_QROWS
    dr_idx = np.full((3, n_kr, NA_QROWS), n_dr, np.int32)
    for case, r0 in enumerate((0, NA_QROWS, rows - NA_QROWS)):
        ks = min(max(r0 - NA_KH // 2, 0), rows - n_kr)
        for a in range(n_kr):
            for b in range(NA_QROWS):
                kr, r = ks + a, r0 + b
                rs = min(max(r - NA_KH // 2, 0), rows - NA_KH)
                if rs <= kr < rs + NA_KH:
                    dr_idx[case, a, b] = kr - r + NA_KH - 1
    t = jnp.full((heads, 3, n_kr, GRID_W, NA_QROWS, GRID_W), NEG, F32)
    for dri in range(n_dr):
        sel = jnp.asarray(dr_idx == dri)[None, :, :, None, :, None]
        t = jnp.where(sel, toe[:, dri][:, None, None, :, None, :], t)
    return t.reshape(heads, 3, n_kr * GRID_W, NA_QROWS * GRID_W)


def _na_attention(q, k, v, bias, batch, tps):
    heads = q.shape[1] // KPAD
    nq = tps
    per_b = tps + 1
    assert nq >= NA_KBLK and nq % 2 == 0
    return pl.pallas_call(
        functools.partial(_na_kernel, nq=nq),
        grid=(batch, heads),
        in_specs=[pl.BlockSpec((nq, KPAD, TM), lambda b, h: (b, h, 0)),
                  pl.BlockSpec((per_b * TM, KPAD), lambda b, h: (b, h // 2)),
                  pl.BlockSpec((per_b, HEAD_DIM, TM), lambda b, h: (b, h, 0)),
                  pl.BlockSpec((1, 3, NA_KBLK * TM, TM), lambda b, h: (h, 0, 0, 0))],
        out_specs=pl.BlockSpec((nq, HEAD_DIM, TM), lambda b, h: (b, h, 0)),
        out_shape=jax.ShapeDtypeStruct((batch * nq, heads * HEAD_DIM, TM), BF16),
        scratch_shapes=[pltpu.VMEM((2, (NA_KBLK + 1) * TM, TM), F32)],
        compiler_params=_cparams(2),
        name="neighbourhood_attention",
    )(q, k, v, bias)


def _merge_mlp_kernel(*refs, final):
    (x_ref, mod_ref, g1_ref, g2_ref, ya_ref, yb_ref, yc_ref, wg_ref, woa_ref, wob_ref, woc_ref,
     wout_ref, w1_ref, w2_ref) = refs[:14]
    o_ref = refs[-1]
    d = D_MODEL
    x = x_ref[...]
    m = mod_ref[0]
    h = _rms_mod(x, g1_ref[...], m[:, d:2 * d], m[:, 0:d]).astype(BF16)
    y = None
    for i, (y_ref, wo_ref) in enumerate(((ya_ref, woa_ref), (yb_ref, wob_ref), (yc_ref, woc_ref))):
        gate = jax.nn.sigmoid(jnp.dot(h, wg_ref[:, i * d:(i + 1) * d], preferred_element_type=F32))
        yt = y_ref[0].astype(F32).T.astype(BF16)
        u = gate * jnp.dot(yt, wo_ref[...], preferred_element_type=F32)
        y = u if y is None else y + u
    a = jnp.dot(y.astype(BF16), wout_ref[...], preferred_element_type=F32)
    x = x + m[:, 2 * d:3 * d] * a

    h = _rms_mod(x, g2_ref[...], m[:, 4 * d:5 * d], m[:, 3 * d:4 * d]).astype(BF16)
    a = None
    for c in range(D_FF // d):
        u = jnp.dot(h, w1_ref[:, c * d:(c + 1) * d], preferred_element_type=F32)
        u = jnp.square(jnp.maximum(u, 0.0)).astype(BF16)
        t = jnp.dot(u, w2_ref[c * d:(c + 1) * d, :], preferred_element_type=F32)
        a = t if a is None else a + t
    out = x + m[:, 5 * d:6 * d] * a
    if final:
        gf = refs[14][...]
        ms = jnp.mean(out * out, axis=-1, keepdims=True)
        out = out * lax.rsqrt(ms + EPS) * gf
    o_ref[...] = out


def _merge_mlp(tok, mod, mod_row, g1, g2, ya, yb, yc, weights, gf):
    t, d = tok.shape
    final = gf is not None
    y_spec = pl.BlockSpec((1, ya.shape[1], TM), lambda i: (i, 0, 0))
    in_specs = [pl.BlockSpec((TM, d), lambda i: (i, 0)),
                pl.BlockSpec((1, 1, 6 * d), lambda i: (mod_row(i), 0, 0)),
                _const_spec((1, d)), _const_spec((1, d)),
                y_spec, y_spec, y_spec] + [_const_spec(w.shape) for w in weights]
    args = [tok, mod, g1, g2, ya, yb, yc, *weights]
    if final:
        in_specs.append(_const_spec((1, d)))
        args.append(gf)
    return pl.pallas_call(
        functools.partial(_merge_mlp_kernel, final=final),
        grid=(t // TM,),
        in_specs=in_specs,
        out_specs=pl.BlockSpec((TM, d), lambda i: (i, 0)),
        out_shape=jax.ShapeDtypeStruct((t, d), F32),
        compiler_params=_cparams(1),
        name="merge_mlp_final" if final else "merge_mlp",
    )(*args)


def _rope_tables(n_tokens, rot_dim):
    t = jnp.arange(n_tokens, dtype=jnp.int32)
    row = (t // GRID_W).astype(F32)
    col = (t % GRID_W).astype(F32)
    half = rot_dim // 2
    inv_freq = ROPE_THETA ** (-jnp.arange(0, half, 2, dtype=F32) / half)
    ar, ac = row[:, None] * inv_freq, col[:, None] * inv_freq
    cr, sr, cc, sn = jnp.cos(ar), jnp.sin(ar), jnp.cos(ac), jnp.sin(ac)
    cos = jnp.concatenate([cr, cr, cc, cc], axis=-1).T
    sin = jnp.concatenate([-sr, sr, -sn, sn], axis=-1).T
    cos = jnp.concatenate([cos, jnp.ones((rot_dim, TM), F32)], axis=1)
    sin = jnp.concatenate([sin, jnp.zeros((rot_dim, TM), F32)], axis=1)
    return cos, sin


def kernel(x, c, ctx, c_ctx, w_mod, b_mod, norm1_g, norm2_g, w_in, gqa_q_norm, gqa_k_norm, na_rpb,
           mla_kv_norm, mla_w_uk, mla_w_uv, w_o_gqa, w_o_na, w_o_mla, w_out, w_mlp1, w_mlp2,
           final_norm_g):
    batch, seq, d = x.shape
    n_ctx = ctx.shape[1]
    depth = w_mod.shape[0]
    assert d == D_MODEL and seq % (NA_QROWS * GRID_W) == 0 and n_ctx == TM
    rows = seq // GRID_W
    assert rows >= NA_KBLK * NA_QROWS
    tps = seq // TM
    tq = min(TQ, seq)

    cond = jnp.concatenate([c, c_ctx[None, :]], axis=0)
    cond = jnp.pad(cond, ((0, -(batch + 1) % 8), (0, 0)))
    mod_all = _modulation(cond, w_mod, b_mod)

    rope = _rope_tables(seq, HEAD_DIM) + _rope_tables(seq, MLA_ROPE)

    xt = x.reshape(batch * seq, d)
    ct = ctx.reshape(batch * n_ctx, d)
    group = GQA_HEADS // GQA_KV_HEADS

    def lat_row(i):
        return i // tps

    def ctx_row(i):
        return batch

    for l in range(depth):
        with_ctx = l < depth - 1
        mod = mod_all[l, :batch + 1].reshape(batch + 1, 1, 6 * d)
        w_t = w_in[l][:, :O_GATE].T.astype(BF16)
        w_g = w_in[l][:, O_GATE:].astype(BF16)
        g1 = norm1_g[l].reshape(1, d)
        g2 = norm2_g[l].reshape(1, d)
        qa, ka, va, qn, kn, vn, qm, km, vm = _project(
            xt, ct, mod, batch, g1, w_t, gqa_q_norm[l].reshape(-1, 1), gqa_k_norm[l].reshape(-1, 1),
            mla_kv_norm[l].reshape(-1, 1), mla_w_uk[l].T.astype(BF16), mla_w_uv[l].T.astype(BF16), rope)
        bias = _na_bias(na_rpb[l], rows)

        att_a = functools.partial(_attention, qa, ka, va, batch, tps, lambda h: 0, lambda h: h // group)
        att_n = functools.partial(_attention, qn, kn, vn, batch, tps, lambda h: h // 2, lambda h: h)
        att_m = functools.partial(_attention, qm, km, vm, batch, tps, lambda h: h, lambda h: h)
        ya = att_a(False, tq)
        yb = _na_attention(qn, kn, vn, bias, batch, tps)
        yc = att_m(False, tq)

        weights = (w_g, w_o_gqa[l].astype(BF16), w_o_na[l].astype(BF16), w_o_mla[l].astype(BF16),
                   w_out[l].astype(BF16), w_mlp1[l].astype(BF16), w_mlp2[l].astype(BF16))
        gf = None if with_ctx else final_norm_g.reshape(1, d)
        xt = _merge_mlp(xt, mod, lat_row, g1, g2, ya, yb, yc, weights, gf)
        if with_ctx:
            ct = _merge_mlp(ct, mod, ctx_row, g1, g2, att_a(True), att_n(True), att_m(True), weights, None)

    return xt.reshape(batch, seq, d)
```

```python
import functools
import math

import numpy as np
import jax
import jax.numpy as jnp
from jax import lax
from jax.experimental import pallas as pl
from jax.experimental.pallas import tpu as pltpu

F32 = jnp.float32
BF16 = jnp.bfloat16

D_MODEL = 1024
GRID_W = 64
HEAD_DIM = 64
GQA_HEADS = 8
GQA_KV_HEADS = 2
NA_HEADS = 8
NA_KH = 8
NA_KW = 16
MLA_HEADS = 8
MLA_NOPE = 64
MLA_ROPE = 32
MLA_QK = MLA_NOPE + MLA_ROPE
MLA_KV_RANK = 256
D_FF = 4 * D_MODEL
ROPE_THETA = 10000.0
EPS = 1e-6
LOG2E = math.log2(math.e)

O_QA = 0
O_KA = O_QA + GQA_HEADS * HEAD_DIM
O_VA = O_KA + GQA_KV_HEADS * HEAD_DIM
O_QN = O_VA + GQA_KV_HEADS * HEAD_DIM
O_KN = O_QN + NA_HEADS * HEAD_DIM
O_VN = O_KN + NA_HEADS * HEAD_DIM
O_QM = O_VN + NA_HEADS * HEAD_DIM
O_CKV = O_QM + MLA_HEADS * MLA_QK
O_KR = O_CKV + MLA_KV_RANK
O_GATE = O_KR + MLA_ROPE

TM = 256
TQ = 1024
ATTN_KC = 1
KPAD = 128
ONES_ROWS = 16
NA_QROWS = 4
NA_UNROLL = 10
NA_KBLK = 3
NEG = -1e30
VMEM_LIMIT = 56 * 1024 * 1024

QSCALE_A = HEAD_DIM ** -0.5 * LOG2E
QSCALE_M = MLA_QK ** -0.5 * LOG2E


def _cparams(n_axes):
    return pltpu.CompilerParams(dimension_semantics=("parallel",) * n_axes,
                                vmem_limit_bytes=VMEM_LIMIT)


def _const_spec(shape):
    nd = len(shape)
    return pl.BlockSpec(shape, lambda *_: (0,) * nd, pipeline_mode=pl.Buffered(1))


def _rms_mod(x, g, scale, shift):
    ms = jnp.mean(x * x, axis=-1, keepdims=True)
    return (x * lax.rsqrt(ms + EPS) * g) * (1.0 + scale) + shift


def _rms_fm(x, g):
    ms = jnp.mean(x * x, axis=0, keepdims=True)
    return x * lax.rsqrt(ms + EPS) * g


def _rope_fm(x, cos, sin, blk):
    sw = jnp.concatenate([x[blk:2 * blk], x[0:blk], x[3 * blk:4 * blk], x[2 * blk:3 * blk]], axis=0)
    return x * cos + sw * sin


def _mod_kernel(c_ref, w_ref, b_ref, o_ref):
    c = c_ref[...]
    c = (c * jax.nn.sigmoid(c)).astype(BF16)
    w = w_ref[0].astype(BF16)
    o_ref[0] = jnp.dot(c, w, preferred_element_type=F32) + b_ref[0]


def _modulation(cond, w_mod, b_mod):
    depth, d, n = w_mod.shape
    rows = cond.shape[0]
    tn = 1536
    return pl.pallas_call(
        _mod_kernel,
        grid=(depth, n // tn),
        in_specs=[pl.BlockSpec((rows, d), lambda l, j: (0, 0)),
                  pl.BlockSpec((1, d, tn), lambda l, j: (l, 0, j)),
                  pl.BlockSpec((1, 1, tn), lambda l, j: (l, 0, j))],
        out_specs=pl.BlockSpec((1, rows, tn), lambda l, j: (l, 0, j)),
        out_shape=jax.ShapeDtypeStruct((depth, rows, n), F32),
        compiler_params=_cparams(2),
        name="modulation",
    )(cond, w_mod, b_mod.reshape(depth, 1, n))


def _proj_kernel(x_ref, c_ref, mod_ref, g_ref, w_ref, gq_ref, gk_ref, gkv_ref, wuk_ref, wuv_ref,
                 cosa_ref, sina_ref, cosm_ref, sinm_ref,
                 qa_o, ka_o, va_o, qn_o, kn_o, vn_o, qm_o, km_o, vm_o, *, n_lat):
    d = D_MODEL
    x = jnp.where(pl.program_id(0) < n_lat, x_ref[...], c_ref[...])
    m = mod_ref[0]
    h = _rms_mod(x, g_ref[...], m[:, d:2 * d], m[:, 0:d]).astype(BF16)
    tm = x.shape[0]

    def seg(a, b):
        return lax.dot_general(w_ref[a:b, :], h, (((1,), (1,)), ((), ())),
                               preferred_element_type=F32)

    def rope_a(t):
        return _rope_fm(t, cosa_ref[...], sina_ref[...], HEAD_DIM // 4)

    def rope_m(t):
        return _rope_fm(t, cosm_ref[...], sinm_ref[...], MLA_ROPE // 4)

    zeros64 = jnp.zeros((HEAD_DIM, tm), BF16)

    p = seg(O_QA, O_KA)
    group = GQA_HEADS // GQA_KV_HEADS
    for hh in range(GQA_HEADS):
        q = rope_a(_rms_fm(p[hh * HEAD_DIM:(hh + 1) * HEAD_DIM], gq_ref[...])) * QSCALE_A
        kvh = hh // group
        for j in range(KPAD // HEAD_DIM):
            r0 = hh * KPAD + j * HEAD_DIM
            qa_o[0, r0:r0 + HEAD_DIM, :] = q.astype(BF16) if j == kvh else zeros64

    p = seg(O_KA, O_VA)
    ks = [rope_a(_rms_fm(p[g * HEAD_DIM:(g + 1) * HEAD_DIM], gk_ref[...])) for g in range(GQA_KV_HEADS)]
    ka_o[...] = jnp.concatenate(ks, axis=0).T.astype(BF16)
    va_o[0] = seg(O_VA, O_QN).astype(BF16)

    p = seg(O_QN, O_KN)
    for hh in range(NA_HEADS):
        q = (p[hh * HEAD_DIM:(hh + 1) * HEAD_DIM] * QSCALE_A).astype(BF16)
        for j in range(KPAD // HEAD_DIM):
            r0 = hh * KPAD + j * HEAD_DIM
            qn_o[0, r0:r0 + HEAD_DIM, :] = q if j == hh % 2 else zeros64
    kn_o[...] = seg(O_KN, O_VN).T.astype(BF16)
    vn_o[0] = seg(O_VN, O_QM).astype(BF16)

    p = seg(O_QM, O_CKV)
    zeros_pad = jnp.zeros((KPAD - MLA_QK, tm), BF16)
    for hh in range(MLA_HEADS):
        base = hh * MLA_QK
        r0 = hh * KPAD
        qm_o[0, r0:r0 + MLA_NOPE, :] = (p[base:base + MLA_NOPE] * QSCALE_M).astype(BF16)
        qm_o[0, r0 + MLA_NOPE:r0 + MLA_QK, :] = (rope_m(p[base + MLA_NOPE:base + MLA_QK]) * QSCALE_M).astype(BF16)
        qm_o[0, r0 + MLA_QK:r0 + KPAD, :] = zeros_pad

    ckv = _rms_fm(seg(O_CKV, O_KR), gkv_ref[...]).astype(BF16)
    k_nope = jnp.dot(wuk_ref[...], ckv, preferred_element_type=F32)
    vm_o[0] = jnp.dot(wuv_ref[...], ckv, preferred_element_type=F32).astype(BF16)
    k_rope = rope_m(seg(O_KR, O_GATE))
    zpad = jnp.zeros((KPAD - MLA_QK, tm), F32)
    blocks = []
    for hh in range(MLA_HEADS):
        blocks += [k_nope[hh * MLA_NOPE:(hh + 1) * MLA_NOPE], k_rope, zpad]
    km_o[...] = jnp.concatenate(blocks, axis=0).T.astype(BF16)


def _project(xt, ct, mod, batch, g1, w_t, gq, gk, gkv, wuk_t, wuv_t, rope):
    d = xt.shape[1]
    n_lat = xt.shape[0] // TM
    n_ctx = ct.shape[0] // TM
    assert n_ctx == batch
    tps = n_lat // batch
    nt = n_lat + n_ctx
    t_all = nt * TM

    def is_lat(i):
        return i < n_lat

    def kv_blk(i):
        return jnp.where(is_lat(i), (i // tps) * (tps + 1) + i % tps, (i - n_lat) * (tps + 1) + tps)

    def rope_spec(tab):
        return pl.BlockSpec((tab.shape[0], TM), lambda i: (0, jnp.where(is_lat(i), i % tps, tps)))

    in_specs = [
        pl.BlockSpec((TM, d), lambda i: (jnp.minimum(i, n_lat - 1), 0)),
        pl.BlockSpec((TM, d), lambda i: (jnp.maximum(i - n_lat, 0), 0)),
        pl.BlockSpec((1, 1, 6 * d), lambda i: (jnp.where(is_lat(i), i // tps, batch), 0, 0)),
        _const_spec((1, d)),
        _const_spec(w_t.shape),
        _const_spec((HEAD_DIM, 1)),
        _const_spec((HEAD_DIM, 1)),
        _const_spec((MLA_KV_RANK, 1)),
        _const_spec(wuk_t.shape),
        _const_spec(wuv_t.shape),
    ] + [rope_spec(tab) for tab in rope]
    nh = NA_HEADS * HEAD_DIM
    ga = GQA_KV_HEADS * HEAD_DIM
    out_shape = [
        jax.ShapeDtypeStruct((nt, GQA_HEADS * KPAD, TM), BF16),
        jax.ShapeDtypeStruct((t_all, KPAD), BF16),
        jax.ShapeDtypeStruct((nt, ga, TM), BF16),
        jax.ShapeDtypeStruct((nt, NA_HEADS * KPAD, TM), BF16),
        jax.ShapeDtypeStruct((t_all, nh), BF16),
        jax.ShapeDtypeStruct((nt, nh, TM), BF16),
        jax.ShapeDtypeStruct((nt, MLA_HEADS * KPAD, TM), BF16),
        jax.ShapeDtypeStruct((t_all, MLA_HEADS * KPAD), BF16),
        jax.ShapeDtypeStruct((nt, MLA_HEADS * HEAD_DIM, TM), BF16),
    ]
    out_specs = [
        pl.BlockSpec((1, GQA_HEADS * KPAD, TM), lambda i: (i, 0, 0)),
        pl.BlockSpec((TM, KPAD), lambda i: (kv_blk(i), 0)),
        pl.BlockSpec((1, ga, TM), lambda i: (kv_blk(i), 0, 0)),
        pl.BlockSpec((1, NA_HEADS * KPAD, TM), lambda i: (i, 0, 0)),
        pl.BlockSpec((TM, nh), lambda i: (kv_blk(i), 0)),
        pl.BlockSpec((1, nh, TM), lambda i: (kv_blk(i), 0, 0)),
        pl.BlockSpec((1, MLA_HEADS * KPAD, TM), lambda i: (i, 0, 0)),
        pl.BlockSpec((TM, MLA_HEADS * KPAD), lambda i: (kv_blk(i), 0)),
        pl.BlockSpec((1, MLA_HEADS * HEAD_DIM, TM), lambda i: (kv_blk(i), 0, 0)),
    ]
    return pl.pallas_call(
        functools.partial(_proj_kernel, n_lat=n_lat),
        grid=(nt,),
        in_specs=in_specs,
        out_specs=out_specs,
        out_shape=out_shape,
        compiler_params=_cparams(1),
        name="project",
    )(xt, ct, mod, g1, w_t, gq, gk, gkv, wuk_t, wuv_t, *rope)


def _attn_kernel(q_ref, k_ref, v_ref, o_ref, s_buf, *, n_chunks, nq, tiles):
    tq = tiles * TM
    dv = o_ref.shape[1]
    ones = jnp.ones((ONES_ROWS, TM), BF16)
    neg = jnp.full((1, tq), NEG, F32)
    steps = [(c, min(ATTN_KC, n_chunks - c)) for c in range(0, n_chunks, ATTN_KC)]
    first_slot = 2

    def slot_of(t):
        return first_slot if t == 0 else t % 2

    def load_q(i):
        return jnp.concatenate([q_ref[i * tiles + j] for j in range(tiles)], axis=1)

    def produce(q, t, m_prev):
        c, n = steps[t]
        s = jnp.dot(k_ref[c * TM:(c + n) * TM, :], q, preferred_element_type=F32)
        s_buf[slot_of(t), 0:n * TM, :] = s
        m_new = jnp.maximum(m_prev, jnp.max(s, axis=0, keepdims=True))
        return m_new, jnp.exp2(m_prev - m_new)

    def consume(t, m, alpha, acc):
        c, n = steps[t]
        p = jnp.exp2(s_buf[slot_of(t), 0:n * TM, :] - m).astype(BF16)
        v_ext = jnp.concatenate([jnp.concatenate([v_ref[c + j], ones], axis=0) for j in range(n)], axis=1)
        pv = jnp.dot(v_ext, p, preferred_element_type=F32)
        return pv if acc is None else alpha * acc + pv

    def block(i, carry):
        m, alpha = carry
        q = load_q(i)
        acc = None
        for t in range(len(steps)):
            if t + 1 < len(steps):
                m_next, a_next = produce(q, t + 1, m)
            elif nq > 1:
                m_next, a_next = produce(load_q(jnp.minimum(i + 1, nq - 1)), 0, neg)
            else:
                m_next, a_next = m, alpha
            acc = consume(t, m, alpha, acc)
            m, alpha = m_next, a_next
        out = (acc[0:dv] / acc[dv:dv + 1]).astype(o_ref.dtype)
        for j in range(tiles):
            o_ref[i * tiles + j] = out[:, j * TM:(j + 1) * TM]
        return m, alpha

    carry = produce(load_q(0), 0, neg)
    if nq > 1:
        lax.fori_loop(0, nq, block, carry)
    else:
        block(0, carry)


def _attention(q, k, v, batch, tps, kcol, vrow, ctx_queries, tq=TM):
    heads = q.shape[1] // KPAD
    per_b = tps + 1
    if ctx_queries:
        q_tiles, tq, n_chunks = 1, TM, 1
        q_spec = pl.BlockSpec((1, KPAD, TM), lambda b, h: (batch * tps + b, h, 0))
        k_spec = pl.BlockSpec((TM, KPAD), lambda b, h: (b * per_b + tps, kcol(h)))
        v_spec = pl.BlockSpec((1, HEAD_DIM, TM), lambda b, h: (b * per_b + tps, vrow(h), 0))
    else:
        q_tiles, n_chunks = tps, per_b
        q_spec = pl.BlockSpec((tps, KPAD, TM), lambda b, h: (b, h, 0))
        k_spec = pl.BlockSpec((per_b * TM, KPAD), lambda b, h: (b, kcol(h)))
        v_spec = pl.BlockSpec((per_b, HEAD_DIM, TM), lambda b, h: (b, vrow(h), 0))
    tiles = tq // TM
    return pl.pallas_call(
        functools.partial(_attn_kernel, n_chunks=n_chunks, nq=q_tiles // tiles, tiles=tiles),
        grid=(batch, heads),
        in_specs=[q_spec, k_spec, v_spec],
        out_specs=pl.BlockSpec((q_tiles, HEAD_DIM, TM), lambda b, h: (b, h, 0)),
        out_shape=jax.ShapeDtypeStruct((batch * q_tiles, heads * HEAD_DIM, TM), BF16),
        scratch_shapes=[pltpu.VMEM((3, min(ATTN_KC, n_chunks) * TM, tq), F32)],
        compiler_params=_cparams(2),
        name="ctx_attention" if ctx_queries else "attention",
    )(q, k, v)


def _na_kernel(q_ref, k_ref, v_ref, toe_ref, o_ref, s_buf, bias_buf, *, nq, dr_idx):
    win = NA_KBLK * TM
    ones = jnp.ones((ONES_ROWS, TM), BF16)
    static = lambda i: isinstance(i, int)

    lane_blk = lax.broadcasted_iota(jnp.int32, (GRID_W, TM), 1) // GRID_W
    masked = jnp.full((GRID_W, TM), NEG, F32)
    strip = lambda d: masked if d < 0 else toe_ref[0, d]
    for case in range(3):
        for a in range(NA_KBLK * NA_QROWS):
            ids = dr_idx[case][a]
            row = strip(ids[0])
            for b in range(1, NA_QROWS):
                if any(ids[j] != ids[0] for j in range(1, b + 1)):
                    row = jnp.where(lane_blk == b, strip(ids[b]), row)
            bias_buf[case, a * GRID_W:(a + 1) * GRID_W, :] = row

    def kblk(i):
        return min(max(i - 1, 0), nq - NA_KBLK) if static(i) else jnp.clip(i - 1, 0, nq - NA_KBLK)

    def produce(i, slot):
        if static(i):
            case, off = (0 if i == 0 else 2 if i == nq - 1 else 1), kblk(i) * TM
        else:
            case = jnp.where(i == 0, 0, jnp.where(i == nq - 1, 2, 1))
            off = pl.multiple_of(kblk(i) * TM, TM)
        q = q_ref[i]
        s_win = jnp.dot(k_ref[pl.ds(off, win), :], q, preferred_element_type=F32) + bias_buf[case]
        s_ctx = jnp.dot(k_ref[pl.ds(nq * TM, TM), :], q, preferred_element_type=F32)
        s_buf[slot, 0:win, :] = s_win
        s_buf[slot, win:win + TM, :] = s_ctx
        return jnp.maximum(jnp.max(s_win, axis=0, keepdims=True), jnp.max(s_ctx, axis=0, keepdims=True))

    def consume(i, slot, m):
        p = jnp.exp2(s_buf[slot] - m).astype(BF16)
        kb = kblk(i)
        chunks = [v_ref[kb + j] for j in range(NA_KBLK)] + [v_ref[nq]]
        v_all = jnp.concatenate([jnp.concatenate([c, ones], axis=0) for c in chunks], axis=1)
        acc = jnp.dot(v_all, p, preferred_element_type=F32)
        dv = o_ref.shape[1]
        o_ref[i] = (acc[0:dv] / acc[dv:dv + 1]).astype(o_ref.dtype)

    unroll = max(u for u in range(2, NA_UNROLL + 1, 2) if (nq - 2) % u == 0)
    m = produce(0, 0)

    def body(it, m):
        c = unroll * it
        for j in range(unroll):
            m_next = produce(c + j + 1, (j + 1) % 2)
            consume(c + j, j % 2, m)
            m = m_next
        return m

    m = lax.fori_loop(0, (nq - 2) // unroll, body, m)
    m_last = produce(nq - 1, (nq - 1) % 2)
    consume(nq - 2, (nq - 2) % 2, m)
    consume(nq - 1, (nq - 1) % 2, m_last)


def _na_toeplitz(rpb):
    kc = np.arange(GRID_W)[:, None]
    c = np.arange(GRID_W)[None, :]
    cs = np.clip(c - NA_KW // 2, 0, GRID_W - NA_KW)
    valid_c = np.tile((kc >= cs) & (kc < cs + NA_KW), (1, NA_QROWS))
    ci = np.tile(kc - c + (NA_KW - 1), (1, NA_QROWS))
    rpb = rpb.astype(F32) * LOG2E
    toe = jnp.full(rpb.shape[:-1] + (GRID_W, TM), NEG, F32)
    for j in range(2 * NA_KW - 1):
        toe = jnp.where(jnp.asarray(valid_c & (ci == j)), rpb[..., j][..., None, None], toe)
    return toe


def _na_row_index(rows):
    n_kr = NA_KBLK * NA_QROWS
    table = []
    for r0 in (0, NA_QROWS, rows - NA_QROWS):
        ks = min(max(r0 - NA_KH // 2, 0), rows - n_kr)
        per_case = []
        for a in range(n_kr):
            ids = []
            for b in range(NA_QROWS):
                kr, r = ks + a, r0 + b
                rs = min(max(r - NA_KH // 2, 0), rows - NA_KH)
                ids.append(kr - r + NA_KH - 1 if rs <= kr < rs + NA_KH else -1)
            per_case.append(tuple(ids))
        table.append(tuple(per_case))
    return tuple(table)


def _na_attention(q, k, v, toe, batch, tps):
    heads = q.shape[1] // KPAD
    nq = tps
    per_b = tps + 1
    assert nq >= NA_KBLK and nq % 2 == 0
    rows = nq * NA_QROWS
    return pl.pallas_call(
        functools.partial(_na_kernel, nq=nq, dr_idx=_na_row_index(rows)),
        grid=(batch, heads),
        in_specs=[pl.BlockSpec((nq, KPAD, TM), lambda b, h: (b, h, 0)),
                  pl.BlockSpec((per_b * TM, KPAD), lambda b, h: (b, h // 2)),
                  pl.BlockSpec((per_b, HEAD_DIM, TM), lambda b, h: (b, h, 0)),
                  pl.BlockSpec((1,) + toe.shape[1:], lambda b, h: (h, 0, 0, 0))],
        out_specs=pl.BlockSpec((nq, HEAD_DIM, TM), lambda b, h: (b, h, 0)),
        out_shape=jax.ShapeDtypeStruct((batch * nq, heads * HEAD_DIM, TM), BF16),
        scratch_shapes=[pltpu.VMEM((2, (NA_KBLK + 1) * TM, TM), F32),
                        pltpu.VMEM((3, NA_KBLK * TM, TM), F32)],
        compiler_params=_cparams(2),
        name="neighbourhood_attention",
    )(q, k, v, toe)


def _merge_mlp_kernel(*refs, final):
    (x_ref, mod_ref, g1_ref, g2_ref, ya_ref, yb_ref, yc_ref, wg_ref, woa_ref, wob_ref, woc_ref,
     wout_ref, w1_ref, w2_ref) = refs[:14]
    o_ref = refs[-1]
    d = D_MODEL
    x = x_ref[...]
    m = mod_ref[0]
    h = _rms_mod(x, g1_ref[...], m[:, d:2 * d], m[:, 0:d]).astype(BF16)
    y = None
    for i, (y_ref, wo_ref) in enumerate(((ya_ref, woa_ref), (yb_ref, wob_ref), (yc_ref, woc_ref))):
        gate = jax.nn.sigmoid(jnp.dot(h, wg_ref[:, i * d:(i + 1) * d], preferred_element_type=F32))
        yt = y_ref[0].astype(F32).T.astype(BF16)
        u = gate * jnp.dot(yt, wo_ref[...], preferred_element_type=F32)
        y = u if y is None else y + u
    a = jnp.dot(y.astype(BF16), wout_ref[...], preferred_element_type=F32)
    x = x + m[:, 2 * d:3 * d] * a

    h = _rms_mod(x, g2_ref[...], m[:, 4 * d:5 * d], m[:, 3 * d:4 * d]).astype(BF16)
    a = None
    for c in range(D_FF // d):
        u = jnp.dot(h, w1_ref[:, c * d:(c + 1) * d], preferred_element_type=F32)
        u = jnp.square(jnp.maximum(u, 0.0)).astype(BF16)
        t = jnp.dot(u, w2_ref[c * d:(c + 1) * d, :], preferred_element_type=F32)
        a = t if a is None else a + t
    out = x + m[:, 5 * d:6 * d] * a
    if final:
        gf = refs[14][...]
        ms = jnp.mean(out * out, axis=-1, keepdims=True)
        out = out * lax.rsqrt(ms + EPS) * gf
    o_ref[...] = out


def _merge_mlp(tok, mod, mod_row, g1, g2, ya, yb, yc, weights, gf):
    t, d = tok.shape
    final = gf is not None
    y_spec = pl.BlockSpec((1, ya.shape[1], TM), lambda i: (i, 0, 0))
    in_specs = [pl.BlockSpec((TM, d), lambda i: (i, 0)),
                pl.BlockSpec((1, 1, 6 * d), lambda i: (mod_row(i), 0, 0)),
                _const_spec((1, d)), _const_spec((1, d)),
                y_spec, y_spec, y_spec] + [_const_spec(w.shape) for w in weights]
    args = [tok, mod, g1, g2, ya, yb, yc, *weights]
    if final:
        in_specs.append(_const_spec((1, d)))
        args.append(gf)
    return pl.pallas_call(
        functools.partial(_merge_mlp_kernel, final=final),
        grid=(t // TM,),
        in_specs=in_specs,
        out_specs=pl.BlockSpec((TM, d), lambda i: (i, 0)),
        out_shape=jax.ShapeDtypeStruct((t, d), F32),
        compiler_params=_cparams(1),
        name="merge_mlp_final" if final else "merge_mlp",
    )(*args)


def _rope_tables(n_tokens, rot_dim):
    t = jnp.arange(n_tokens, dtype=jnp.int32)
    row = (t // GRID_W).astype(F32)
    col = (t % GRID_W).astype(F32)
    half = rot_dim // 2
    inv_freq = ROPE_THETA ** (-jnp.arange(0, half, 2, dtype=F32) / half)
    ar, ac = row[:, None] * inv_freq, col[:, None] * inv_freq
    cr, sr, cc, sn = jnp.cos(ar), jnp.sin(ar), jnp.cos(ac), jnp.sin(ac)
    cos = jnp.concatenate([cr, cr, cc, cc], axis=-1).T
    sin = jnp.concatenate([-sr, sr, -sn, sn], axis=-1).T
    cos = jnp.concatenate([cos, jnp.ones((rot_dim, TM), F32)], axis=1)
    sin = jnp.concatenate([sin, jnp.zeros((rot_dim, TM), F32)], axis=1)
    return cos, sin


def kernel(x, c, ctx, c_ctx, w_mod, b_mod, norm1_g, norm2_g, w_in, gqa_q_norm, gqa_k_norm, na_rpb,
           mla_kv_norm, mla_w_uk, mla_w_uv, w_o_gqa, w_o_na, w_o_mla, w_out, w_mlp1, w_mlp2,
           final_norm_g):
    batch, seq, d = x.shape
    n_ctx = ctx.shape[1]
    depth = w_mod.shape[0]
    assert d == D_MODEL and seq % (NA_QROWS * GRID_W) == 0 and n_ctx == TM
    rows = seq // GRID_W
    assert rows >= NA_KBLK * NA_QROWS
    tps = seq // TM
    tq = min(TQ, seq)

    cond = jnp.concatenate([c, c_ctx[None, :]], axis=0)
    cond = jnp.pad(cond, ((0, -(batch + 1) % 8), (0, 0)))
    mod_all = _modulation(cond, w_mod, b_mod)

    rope = _rope_tables(seq, HEAD_DIM) + _rope_tables(seq, MLA_ROPE)
    toe_all = _na_toeplitz(na_rpb)

    xt = x.reshape(batch * seq, d)
    ct = ctx.reshape(batch * n_ctx, d)
    group = GQA_HEADS // GQA_KV_HEADS

    def lat_row(i):
        return i // tps

    def ctx_row(i):
        return batch

    for l in range(depth):
        with_ctx = l < depth - 1
        mod = mod_all[l, :batch + 1].reshape(batch + 1, 1, 6 * d)
        w_t = w_in[l][:, :O_GATE].T.astype(BF16)
        w_g = w_in[l][:, O_GATE:].astype(BF16)
        g1 = norm1_g[l].reshape(1, d)
        g2 = norm2_g[l].reshape(1, d)
        qa, ka, va, qn, kn, vn, qm, km, vm = _project(
            xt, ct, mod, batch, g1, w_t, gqa_q_norm[l].reshape(-1, 1), gqa_k_norm[l].reshape(-1, 1),
            mla_kv_norm[l].reshape(-1, 1), mla_w_uk[l].T.astype(BF16), mla_w_uv[l].T.astype(BF16), rope)

        att_a = functools.partial(_attention, qa, ka, va, batch, tps, lambda h: 0, lambda h: h // group)
        att_n = functools.partial(_attention, qn, kn, vn, batch, tps, lambda h: h // 2, lambda h: h)
        att_m = functools.partial(_attention, qm, km, vm, batch, tps, lambda h: h, lambda h: h)
        ya = att_a(False, tq)
        yb = _na_attention(qn, kn, vn, toe_all[l], batch, tps)
        yc = att_m(False, tq)

        weights = (w_g, w_o_gqa[l].astype(BF16), w_o_na[l].astype(BF16), w_o_mla[l].astype(BF16),
                   w_out[l].astype(BF16), w_mlp1[l].astype(BF16), w_mlp2[l].astype(BF16))
        gf = None if with_ctx else final_norm_g.reshape(1, d)
        xt = _merge_mlp(xt, mod, lat_row, g1, g2, ya, yb, yc, weights, gf)
        if with_ctx:
            ct = _merge_mlp(ct, mod, ctx_row, g1, g2, att_a(True), att_n(True), att_m(True), weights, None)

    return xt.reshape(batch, seq, d)
```

```python
import functools
import math

import numpy as np
import jax
import jax.numpy as jnp
from jax import lax
from jax.experimental import pallas as pl
from jax.experimental.pallas import tpu as pltpu

F32 = jnp.float32
BF16 = jnp.bfloat16

D_MODEL = 1024
GRID_W = 64
HEAD_DIM = 64
GQA_HEADS = 8
GQA_KV_HEADS = 2
NA_HEADS = 8
NA_KH = 8
NA_KW = 16
MLA_HEADS = 8
MLA_NOPE = 64
MLA_ROPE = 32
MLA_QK = MLA_NOPE + MLA_ROPE
MLA_KV_RANK = 256
D_FF = 4 * D_MODEL
ROPE_THETA = 10000.0
EPS = 1e-6
LOG2E = math.log2(math.e)

O_QA = 0
O_KA = O_QA + GQA_HEADS * HEAD_DIM
O_VA = O_KA + GQA_KV_HEADS * HEAD_DIM
O_QN = O_VA + GQA_KV_HEADS * HEAD_DIM
O_KN = O_QN + NA_HEADS * HEAD_DIM
O_VN = O_KN + NA_HEADS * HEAD_DIM
O_QM = O_VN + NA_HEADS * HEAD_DIM
O_CKV = O_QM + MLA_HEADS * MLA_QK
O_KR = O_CKV + MLA_KV_RANK
O_GATE = O_KR + MLA_ROPE

TM = 256
TQ = 1024
ATTN_KC = 1
KPAD = 128
ONES_ROWS = 16
NA_QROWS = 4
MM_TILES = 2
NA_UNROLL = 10
NA_KBLK = 3
NEG = -1e30
VMEM_LIMIT = 56 * 1024 * 1024

QSCALE_A = HEAD_DIM ** -0.5 * LOG2E
QSCALE_M = MLA_QK ** -0.5 * LOG2E


def _cparams(n_axes):
    return pltpu.CompilerParams(dimension_semantics=("parallel",) * n_axes,
                                vmem_limit_bytes=VMEM_LIMIT)


def _layer_spec(stacked, layer):
    idx = (layer,) + (0,) * (stacked.ndim - 1)
    return pl.BlockSpec((None,) + stacked.shape[1:], lambda *_: idx, pipeline_mode=pl.Buffered(1))


def _const_spec(shape):
    nd = len(shape)
    return pl.BlockSpec(shape, lambda *_: (0,) * nd, pipeline_mode=pl.Buffered(1))


def _rms_mod(x, g, scale, shift):
    ms = jnp.mean(x * x, axis=-1, keepdims=True)
    return (x * lax.rsqrt(ms + EPS) * g) * (1.0 + scale) + shift


def _rms_fm(x, g):
    ms = jnp.mean(x * x, axis=0, keepdims=True)
    return x * lax.rsqrt(ms + EPS) * g


def _rope_fm(x, cos, sin, blk):
    sw = jnp.concatenate([x[blk:2 * blk], x[0:blk], x[3 * blk:4 * blk], x[2 * blk:3 * blk]], axis=0)
    return x * cos + sw * sin


def _mod_kernel(c_ref, w_ref, b_ref, o_ref):
    c = c_ref[...]
    c = (c * jax.nn.sigmoid(c)).astype(BF16)
    w = w_ref[0].astype(BF16)
    o_ref[0] = jnp.dot(c, w, preferred_element_type=F32) + b_ref[0]


def _modulation(cond, w_mod, b_mod):
    depth, d, n = w_mod.shape
    rows = cond.shape[0]
    tn = 1536
    return pl.pallas_call(
        _mod_kernel,
        grid=(depth, n // tn),
        in_specs=[pl.BlockSpec((rows, d), lambda l, j: (0, 0)),
                  pl.BlockSpec((1, d, tn), lambda l, j: (l, 0, j)),
                  pl.BlockSpec((1, 1, tn), lambda l, j: (l, 0, j))],
        out_specs=pl.BlockSpec((1, rows, tn), lambda l, j: (l, 0, j)),
        out_shape=jax.ShapeDtypeStruct((depth, rows, n), F32),
        compiler_params=_cparams(2),
        name="modulation",
    )(cond, w_mod, b_mod.reshape(depth, 1, n))


def _proj_kernel(x_ref, c_ref, mod_ref, g_ref, w_ref, gq_ref, gk_ref, gkv_ref, wuk_ref, wuv_ref,
                 cosa_ref, sina_ref, cosm_ref, sinm_ref,
                 qa_o, ka_o, va_o, qn_o, kn_o, vn_o, qm_o, km_o, vm_o, *, n_lat):
    d = D_MODEL
    x = jnp.where(pl.program_id(0) < n_lat, x_ref[...], c_ref[...])
    m = mod_ref[0]
    h = _rms_mod(x, g_ref[...], m[:, d:2 * d], m[:, 0:d]).astype(BF16)
    tm = x.shape[0]

    def seg(a, b):
        return lax.dot_general(w_ref[a:b, :], h, (((1,), (1,)), ((), ())),
                               preferred_element_type=F32)

    def rope_a(t):
        return _rope_fm(t, cosa_ref[...], sina_ref[...], HEAD_DIM // 4)

    def rope_m(t):
        return _rope_fm(t, cosm_ref[...], sinm_ref[...], MLA_ROPE // 4)

    zeros64 = jnp.zeros((HEAD_DIM, tm), BF16)

    p = seg(O_QA, O_KA)
    group = GQA_HEADS // GQA_KV_HEADS
    for hh in range(GQA_HEADS):
        q = rope_a(_rms_fm(p[hh * HEAD_DIM:(hh + 1) * HEAD_DIM], gq_ref[...])) * QSCALE_A
        kvh = hh // group
        for j in range(KPAD // HEAD_DIM):
            r0 = hh * KPAD + j * HEAD_DIM
            qa_o[0, r0:r0 + HEAD_DIM, :] = q.astype(BF16) if j == kvh else zeros64

    p = seg(O_KA, O_VA)
    ks = [rope_a(_rms_fm(p[g * HEAD_DIM:(g + 1) * HEAD_DIM], gk_ref[...])) for g in range(GQA_KV_HEADS)]
    ka_o[...] = jnp.concatenate(ks, axis=0).T.astype(BF16)
    va_o[0] = seg(O_VA, O_QN).astype(BF16)

    p = seg(O_QN, O_KN)
    for hh in range(NA_HEADS):
        q = (p[hh * HEAD_DIM:(hh + 1) * HEAD_DIM] * QSCALE_A).astype(BF16)
        for j in range(KPAD // HEAD_DIM):
            r0 = hh * KPAD + j * HEAD_DIM
            qn_o[0, r0:r0 + HEAD_DIM, :] = q if j == hh % 2 else zeros64
    kn_o[...] = seg(O_KN, O_VN).T.astype(BF16)
    vn_o[0] = seg(O_VN, O_QM).astype(BF16)

    p = seg(O_QM, O_CKV)
    zeros_pad = jnp.zeros((KPAD - MLA_QK, tm), BF16)
    for hh in range(MLA_HEADS):
        base = hh * MLA_QK
        r0 = hh * KPAD
        qm_o[0, r0:r0 + MLA_NOPE, :] = (p[base:base + MLA_NOPE] * QSCALE_M).astype(BF16)
        qm_o[0, r0 + MLA_NOPE:r0 + MLA_QK, :] = (rope_m(p[base + MLA_NOPE:base + MLA_QK]) * QSCALE_M).astype(BF16)
        qm_o[0, r0 + MLA_QK:r0 + KPAD, :] = zeros_pad

    ckv = _rms_fm(seg(O_CKV, O_KR), gkv_ref[...]).astype(BF16)
    k_nope = jnp.dot(wuk_ref[...], ckv, preferred_element_type=F32)
    vm_o[0] = jnp.dot(wuv_ref[...], ckv, preferred_element_type=F32).astype(BF16)
    k_rope = rope_m(seg(O_KR, O_GATE))
    zpad = jnp.zeros((KPAD - MLA_QK, tm), F32)
    blocks = []
    for hh in range(MLA_HEADS):
        blocks += [k_nope[hh * MLA_NOPE:(hh + 1) * MLA_NOPE], k_rope, zpad]
    km_o[...] = jnp.concatenate(blocks, axis=0).T.astype(BF16)


def _project(xt, ct, mod, layer, batch, g1, w_t, gq, gk, gkv, wuk_t, wuv_t, rope):
    d = xt.shape[1]
    n_lat = xt.shape[0] // TM
    n_ctx = ct.shape[0] // TM
    assert n_ctx == batch
    tps = n_lat // batch
    nt = n_lat + n_ctx
    t_all = nt * TM

    def is_lat(i):
        return i < n_lat

    def kv_blk(i):
        return jnp.where(is_lat(i), (i // tps) * (tps + 1) + i % tps, (i - n_lat) * (tps + 1) + tps)

    def rope_spec(tab):
        return pl.BlockSpec((tab.shape[0], TM), lambda i: (0, jnp.where(is_lat(i), i % tps, tps)))

    in_specs = [
        pl.BlockSpec((TM, d), lambda i: (jnp.minimum(i, n_lat - 1), 0)),
        pl.BlockSpec((TM, d), lambda i: (jnp.maximum(i - n_lat, 0), 0)),
        pl.BlockSpec((None, 1, 1, 6 * d), lambda i: (layer, jnp.where(is_lat(i), i // tps, batch), 0, 0)),
    ] + [_layer_spec(a, layer) for a in (g1, w_t, gq, gk, gkv, wuk_t, wuv_t)] + [rope_spec(tab) for tab in rope]
    nh = NA_HEADS * HEAD_DIM
    ga = GQA_KV_HEADS * HEAD_DIM
    out_shape = [
        jax.ShapeDtypeStruct((nt, GQA_HEADS * KPAD, TM), BF16),
        jax.ShapeDtypeStruct((t_all, KPAD), BF16),
        jax.ShapeDtypeStruct((nt, ga, TM), BF16),
        jax.ShapeDtypeStruct((nt, NA_HEADS * KPAD, TM), BF16),
        jax.ShapeDtypeStruct((t_all, nh), BF16),
        jax.ShapeDtypeStruct((nt, nh, TM), BF16),
        jax.ShapeDtypeStruct((nt, MLA_HEADS * KPAD, TM), BF16),
        jax.ShapeDtypeStruct((t_all, MLA_HEADS * KPAD), BF16),
        jax.ShapeDtypeStruct((nt, MLA_HEADS * HEAD_DIM, TM), BF16),
    ]
    out_specs = [
        pl.BlockSpec((1, GQA_HEADS * KPAD, TM), lambda i: (i, 0, 0)),
        pl.BlockSpec((TM, KPAD), lambda i: (kv_blk(i), 0)),
        pl.BlockSpec((1, ga, TM), lambda i: (kv_blk(i), 0, 0)),
        pl.BlockSpec((1, NA_HEADS * KPAD, TM), lambda i: (i, 0, 0)),
        pl.BlockSpec((TM, nh), lambda i: (kv_blk(i), 0)),
        pl.BlockSpec((1, nh, TM), lambda i: (kv_blk(i), 0, 0)),
        pl.BlockSpec((1, MLA_HEADS * KPAD, TM), lambda i: (i, 0, 0)),
        pl.BlockSpec((TM, MLA_HEADS * KPAD), lambda i: (kv_blk(i), 0)),
        pl.BlockSpec((1, MLA_HEADS * HEAD_DIM, TM), lambda i: (kv_blk(i), 0, 0)),
    ]
    return pl.pallas_call(
        functools.partial(_proj_kernel, n_lat=n_lat),
        grid=(nt,),
        in_specs=in_specs,
        out_specs=out_specs,
        out_shape=out_shape,
        compiler_params=_cparams(1),
        name="project",
    )(xt, ct, mod, g1, w_t, gq, gk, gkv, wuk_t, wuv_t, *rope)


def _attn_kernel(q_ref, k_ref, v_ref, o_ref, s_buf, *, n_chunks, nq, tiles):
    tq = tiles * TM
    dv = o_ref.shape[1]
    ones = jnp.ones((ONES_ROWS, TM), BF16)
    neg = jnp.full((1, tq), NEG, F32)
    steps = [(c, min(ATTN_KC, n_chunks - c)) for c in range(0, n_chunks, ATTN_KC)]
    first_slot = 2

    def slot_of(t):
        return first_slot if t == 0 else t % 2

    def load_q(i):
        return jnp.concatenate([q_ref[i * tiles + j] for j in range(tiles)], axis=1)

    def produce(q, t, m_prev):
        c, n = steps[t]
        s = jnp.dot(k_ref[c * TM:(c + n) * TM, :], q, preferred_element_type=F32)
        s_buf[slot_of(t), 0:n * TM, :] = s
        m_new = jnp.maximum(m_prev, jnp.max(s, axis=0, keepdims=True))
        return m_new, jnp.exp2(m_prev - m_new)

    def consume(t, m, alpha, acc):
        c, n = steps[t]
        p = jnp.exp2(s_buf[slot_of(t), 0:n * TM, :] - m).astype(BF16)
        v_ext = jnp.concatenate([jnp.concatenate([v_ref[c + j], ones], axis=0) for j in range(n)], axis=1)
        pv = jnp.dot(v_ext, p, preferred_element_type=F32)
        return pv if acc is None else alpha * acc + pv

    def block(i, carry):
        m, alpha = carry
        q = load_q(i)
        acc = None
        for t in range(len(steps)):
            if t + 1 < len(steps):
                m_next, a_next = produce(q, t + 1, m)
            elif nq > 1:
                m_next, a_next = produce(load_q(jnp.minimum(i + 1, nq - 1)), 0, neg)
            else:
                m_next, a_next = m, alpha
            acc = consume(t, m, alpha, acc)
            m, alpha = m_next, a_next
        out = (acc[0:dv] / acc[dv:dv + 1]).astype(o_ref.dtype)
        for j in range(tiles):
            o_ref[i * tiles + j] = out[:, j * TM:(j + 1) * TM]
        return m, alpha

    carry = produce(load_q(0), 0, neg)
    if nq > 1:
        lax.fori_loop(0, nq, block, carry)
    else:
        block(0, carry)


def _attention(q, k, v, batch, tps, kcol, vrow, ctx_queries, tq=TM):
    heads = q.shape[1] // KPAD
    per_b = tps + 1
    if ctx_queries:
        q_tiles, tq, n_chunks = 1, TM, 1
        q_spec = pl.BlockSpec((1, KPAD, TM), lambda b, h: (batch * tps + b, h, 0))
        k_spec = pl.BlockSpec((TM, KPAD), lambda b, h: (b * per_b + tps, kcol(h)))
        v_spec = pl.BlockSpec((1, HEAD_DIM, TM), lambda b, h: (b * per_b + tps, vrow(h), 0))
    else:
        q_tiles, n_chunks = tps, per_b
        q_spec = pl.BlockSpec((tps, KPAD, TM), lambda b, h: (b, h, 0))
        k_spec = pl.BlockSpec((per_b * TM, KPAD), lambda b, h: (b, kcol(h)))
        v_spec = pl.BlockSpec((per_b, HEAD_DIM, TM), lambda b, h: (b, vrow(h), 0))
    tiles = tq // TM
    return pl.pallas_call(
        functools.partial(_attn_kernel, n_chunks=n_chunks, nq=q_tiles // tiles, tiles=tiles),
        grid=(batch, heads),
        in_specs=[q_spec, k_spec, v_spec],
        out_specs=pl.BlockSpec((q_tiles, HEAD_DIM, TM), lambda b, h: (b, h, 0)),
        out_shape=jax.ShapeDtypeStruct((batch * q_tiles, heads * HEAD_DIM, TM), BF16),
        scratch_shapes=[pltpu.VMEM((3, min(ATTN_KC, n_chunks) * TM, tq), F32)],
        compiler_params=_cparams(2),
        name="ctx_attention" if ctx_queries else "attention",
    )(q, k, v)


def _na_kernel(q_ref, k_ref, v_ref, toe_ref, o_ref, s_buf, bias_buf, *, nq, dr_idx):
    win = NA_KBLK * TM
    ones = jnp.ones((ONES_ROWS, TM), BF16)
    static = lambda i: isinstance(i, int)

    lane_blk = lax.broadcasted_iota(jnp.int32, (GRID_W, TM), 1) // GRID_W
    masked = jnp.full((GRID_W, TM), NEG, F32)
    strip = lambda d: masked if d < 0 else toe_ref[0, d]
    for case in range(3):
        for a in range(NA_KBLK * NA_QROWS):
            ids = dr_idx[case][a]
            row = strip(ids[0])
            for b in range(1, NA_QROWS):
                if any(ids[j] != ids[0] for j in range(1, b + 1)):
                    row = jnp.where(lane_blk == b, strip(ids[b]), row)
            bias_buf[case, a * GRID_W:(a + 1) * GRID_W, :] = row

    def kblk(i):
        return min(max(i - 1, 0), nq - NA_KBLK) if static(i) else jnp.clip(i - 1, 0, nq - NA_KBLK)

    def produce(i, slot):
        if static(i):
            case, off = (0 if i == 0 else 2 if i == nq - 1 else 1), kblk(i) * TM
        else:
            case = jnp.where(i == 0, 0, jnp.where(i == nq - 1, 2, 1))
            off = pl.multiple_of(kblk(i) * TM, TM)
        q = q_ref[i]
        s_win = jnp.dot(k_ref[pl.ds(off, win), :], q, preferred_element_type=F32) + bias_buf[case]
        s_ctx = jnp.dot(k_ref[pl.ds(nq * TM, TM), :], q, preferred_element_type=F32)
        s_buf[slot, 0:win, :] = s_win
        s_buf[slot, win:win + TM, :] = s_ctx
        return jnp.maximum(jnp.max(s_win, axis=0, keepdims=True), jnp.max(s_ctx, axis=0, keepdims=True))

    def consume(i, slot, m):
        p = jnp.exp2(s_buf[slot] - m).astype(BF16)
        kb = kblk(i)
        chunks = [v_ref[kb + j] for j in range(NA_KBLK)] + [v_ref[nq]]
        v_all = jnp.concatenate([jnp.concatenate([c, ones], axis=0) for c in chunks], axis=1)
        acc = jnp.dot(v_all, p, preferred_element_type=F32)
        dv = o_ref.shape[1]
        o_ref[i] = (acc[0:dv] / acc[dv:dv + 1]).astype(o_ref.dtype)

    unroll = max(u for u in range(2, NA_UNROLL + 1, 2) if (nq - 2) % u == 0)
    m = produce(0, 0)

    def body(it, m):
        c = unroll * it
        for j in range(unroll):
            m_next = produce(c + j + 1, (j + 1) % 2)
            consume(c + j, j % 2, m)
            m = m_next
        return m

    m = lax.fori_loop(0, (nq - 2) // unroll, body, m)
    m_last = produce(nq - 1, (nq - 1) % 2)
    consume(nq - 2, (nq - 2) % 2, m)
    consume(nq - 1, (nq - 1) % 2, m_last)


def _na_toeplitz(rpb):
    kc = np.arange(GRID_W)[:, None]
    c = np.arange(GRID_W)[None, :]
    cs = np.clip(c - NA_KW // 2, 0, GRID_W - NA_KW)
    valid_c = np.tile((kc >= cs) & (kc < cs + NA_KW), (1, NA_QROWS))
    ci = np.tile(kc - c + (NA_KW - 1), (1, NA_QROWS))
    rpb = rpb.astype(F32) * LOG2E
    toe = jnp.full(rpb.shape[:-1] + (GRID_W, TM), NEG, F32)
    for j in range(2 * NA_KW - 1):
        toe = jnp.where(jnp.asarray(valid_c & (ci == j)), rpb[..., j][..., None, None], toe)
    return toe


def _na_row_index(rows):
    n_kr = NA_KBLK * NA_QROWS
    table = []
    for r0 in (0, NA_QROWS, rows - NA_QROWS):
        ks = min(max(r0 - NA_KH // 2, 0), rows - n_kr)
        per_case = []
        for a in range(n_kr):
            ids = []
            for b in range(NA_QROWS):
                kr, r = ks + a, r0 + b
                rs = min(max(r - NA_KH // 2, 0), rows - NA_KH)
                ids.append(kr - r + NA_KH - 1 if rs <= kr < rs + NA_KH else -1)
            per_case.append(tuple(ids))
        table.append(tuple(per_case))
    return tuple(table)


def _na_attention(q, k, v, toe, layer, batch, tps):
    heads = q.shape[1] // KPAD
    nq = tps
    per_b = tps + 1
    assert nq >= NA_KBLK and nq % 2 == 0
    rows = nq * NA_QROWS
    return pl.pallas_call(
        functools.partial(_na_kernel, nq=nq, dr_idx=_na_row_index(rows)),
        grid=(batch, heads),
        in_specs=[pl.BlockSpec((nq, KPAD, TM), lambda b, h: (b, h, 0)),
                  pl.BlockSpec((per_b * TM, KPAD), lambda b, h: (b, h // 2)),
                  pl.BlockSpec((per_b, HEAD_DIM, TM), lambda b, h: (b, h, 0)),
                  pl.BlockSpec((None, 1) + toe.shape[2:], lambda b, h: (layer, h, 0, 0, 0))],
        out_specs=pl.BlockSpec((nq, HEAD_DIM, TM), lambda b, h: (b, h, 0)),
        out_shape=jax.ShapeDtypeStruct((batch * nq, heads * HEAD_DIM, TM), BF16),
        scratch_shapes=[pltpu.VMEM((2, (NA_KBLK + 1) * TM, TM), F32),
                        pltpu.VMEM((3, NA_KBLK * TM, TM), F32)],
        compiler_params=_cparams(2),
        name="neighbourhood_attention",
    )(q, k, v, toe)


def _merge_mlp_kernel(*refs, final):
    (x_ref, mod_ref, g1_ref, g2_ref, ya_ref, yb_ref, yc_ref, wg_ref, woa_ref, wob_ref, woc_ref,
     wout_ref, w1_ref, w2_ref) = refs[:14]
    o_ref = refs[-1]
    d = D_MODEL
    x = x_ref[...]
    m = mod_ref[0]
    h = _rms_mod(x, g1_ref[...], m[:, d:2 * d], m[:, 0:d]).astype(BF16)
    y = None
    for i, (y_ref, wo_ref) in enumerate(((ya_ref, woa_ref), (yb_ref, wob_ref), (yc_ref, woc_ref))):
        gate = jax.nn.sigmoid(jnp.dot(h, wg_ref[:, i * d:(i + 1) * d], preferred_element_type=F32))
        yt = jnp.concatenate([y_ref[j].astype(F32).T.astype(BF16) for j in range(y_ref.shape[0])],
                             axis=0)
        u = gate * jnp.dot(yt, wo_ref[...], preferred_element_type=F32)
        y = u if y is None else y + u
    a = jnp.dot(y.astype(BF16), wout_ref[...], preferred_element_type=F32)
    x = x + m[:, 2 * d:3 * d] * a

    h = _rms_mod(x, g2_ref[...], m[:, 4 * d:5 * d], m[:, 3 * d:4 * d]).astype(BF16)
    a = None
    for c in range(D_FF // d):
        u = jnp.dot(h, w1_ref[:, c * d:(c + 1) * d], preferred_element_type=F32)
        u = jnp.square(jnp.maximum(u, 0.0)).astype(BF16)
        t = jnp.dot(u, w2_ref[c * d:(c + 1) * d, :], preferred_element_type=F32)
        a = t if a is None else a + t
    out = x + m[:, 5 * d:6 * d] * a
    if final:
        gf = refs[14][...]
        ms = jnp.mean(out * out, axis=-1, keepdims=True)
        out = out * lax.rsqrt(ms + EPS) * gf
    o_ref[...] = out


def _merge_mlp(tok, mod, layer, mod_row, g1, g2, ya, yb, yc, weights, gf):
    t, d = tok.shape
    final = gf is not None
    ts = MM_TILES * TM
    y_spec = pl.BlockSpec((MM_TILES, ya.shape[1], TM), lambda i: (i, 0, 0))
    in_specs = [pl.BlockSpec((ts, d), lambda i: (i, 0)),
                pl.BlockSpec((None, 1, 1, 6 * d), lambda i: (layer, mod_row(i), 0, 0)),
                _layer_spec(g1, layer), _layer_spec(g2, layer),
                y_spec, y_spec, y_spec] + [_layer_spec(w, layer) for w in weights]
    args = [tok, mod, g1, g2, ya, yb, yc, *weights]
    if final:
        in_specs.append(_const_spec((1, d)))
        args.append(gf)
    return pl.pallas_call(
        functools.partial(_merge_mlp_kernel, final=final),
        grid=(t // ts,),
        in_specs=in_specs,
        out_specs=pl.BlockSpec((ts, d), lambda i: (i, 0)),
        out_shape=jax.ShapeDtypeStruct((t, d), F32),
        compiler_params=_cparams(1),
        name="merge_mlp_final" if final else "merge_mlp",
    )(*args)


def _rope_tables(n_tokens, rot_dim):
    t = jnp.arange(n_tokens, dtype=jnp.int32)
    row = (t // GRID_W).astype(F32)
    col = (t % GRID_W).astype(F32)
    half = rot_dim // 2
    inv_freq = ROPE_THETA ** (-jnp.arange(0, half, 2, dtype=F32) / half)
    ar, ac = row[:, None] * inv_freq, col[:, None] * inv_freq
    cr, sr, cc, sn = jnp.cos(ar), jnp.sin(ar), jnp.cos(ac), jnp.sin(ac)
    cos = jnp.concatenate([cr, cr, cc, cc], axis=-1).T
    sin = jnp.concatenate([-sr, sr, -sn, sn], axis=-1).T
    cos = jnp.concatenate([cos, jnp.ones((rot_dim, TM), F32)], axis=1)
    sin = jnp.concatenate([sin, jnp.zeros((rot_dim, TM), F32)], axis=1)
    return cos, sin


def kernel(x, c, ctx, c_ctx, w_mod, b_mod, norm1_g, norm2_g, w_in, gqa_q_norm, gqa_k_norm, na_rpb,
           mla_kv_norm, mla_w_uk, mla_w_uv, w_o_gqa, w_o_na, w_o_mla, w_out, w_mlp1, w_mlp2,
           final_norm_g):
    batch, seq, d = x.shape
    n_ctx = ctx.shape[1]
    depth = w_mod.shape[0]
    assert d == D_MODEL and seq % (NA_QROWS * GRID_W) == 0 and n_ctx == TM
    rows = seq // GRID_W
    assert rows >= NA_KBLK * NA_QROWS
    tps = seq // TM
    tq = min(TQ, seq)

    cond = jnp.concatenate([c, c_ctx[None, :]], axis=0)
    cond = jnp.pad(cond, ((0, -(batch + 1) % 8), (0, 0)))
    mod_all = _modulation(cond, w_mod, b_mod)

    rope = _rope_tables(seq, HEAD_DIM) + _rope_tables(seq, MLA_ROPE)
    toe_all = _na_toeplitz(na_rpb)

    xt = x.reshape(batch * seq, d)
    ct = ctx.reshape(batch * n_ctx, d)
    group = GQA_HEADS // GQA_KV_HEADS

    def lat_row(i):
        return i // (tps // MM_TILES)

    def ctx_row(i):
        return batch

    mod = mod_all.reshape(depth, -1, 1, 6 * d)
    g1 = norm1_g.reshape(depth, 1, d)
    g2 = norm2_g.reshape(depth, 1, d)
    proj_params = (g1, jnp.swapaxes(w_in[:, :, :O_GATE], 1, 2).astype(BF16),
                   gqa_q_norm.reshape(depth, -1, 1), gqa_k_norm.reshape(depth, -1, 1),
                   mla_kv_norm.reshape(depth, -1, 1),
                   jnp.swapaxes(mla_w_uk, 1, 2).astype(BF16), jnp.swapaxes(mla_w_uv, 1, 2).astype(BF16))
    weights = (w_in[:, :, O_GATE:].astype(BF16), w_o_gqa.astype(BF16), w_o_na.astype(BF16),
               w_o_mla.astype(BF16), w_out.astype(BF16), w_mlp1.astype(BF16), w_mlp2.astype(BF16))

    for l in range(depth):
        with_ctx = l < depth - 1
        qa, ka, va, qn, kn, vn, qm, km, vm = _project(xt, ct, mod, l, batch, *proj_params, rope)

        att_a = functools.partial(_attention, qa, ka, va, batch, tps, lambda h: 0, lambda h: h // group)
        att_n = functools.partial(_attention, qn, kn, vn, batch, tps, lambda h: h // 2, lambda h: h)
        att_m = functools.partial(_attention, qm, km, vm, batch, tps, lambda h: h, lambda h: h)
        ya = att_a(False, tq)
        yb = _na_attention(qn, kn, vn, toe_all, l, batch, tps)
        yc = att_m(False, tq)

        gf = None if with_ctx else final_norm_g.reshape(1, d)
        xt = _merge_mlp(xt, mod, l, lat_row, g1, g2, ya, yb, yc, weights, gf)
        if with_ctx:
            ct = _merge_mlp(ct, mod, l, ctx_row, g1, g2, att_a(True), att_n(True), att_m(True), weights, None)

    return xt.reshape(batch, seq, d)
```

```python
import functools
import math

import numpy as np
import jax
import jax.numpy as jnp
from jax import lax
from jax.experimental import pallas as pl
from jax.experimental.pallas import tpu as pltpu

F32 = jnp.float32
BF16 = jnp.bfloat16

D_MODEL = 1024
GRID_W = 64
HEAD_DIM = 64
GQA_HEADS = 8
GQA_KV_HEADS = 2
NA_HEADS = 8
NA_KH = 8
NA_KW = 16
MLA_HEADS = 8
MLA_NOPE = 64
MLA_ROPE = 32
MLA_QK = MLA_NOPE + MLA_ROPE
MLA_KV_RANK = 256
D_FF = 4 * D_MODEL
ROPE_THETA = 10000.0
EPS = 1e-6
LOG2E = math.log2(math.e)

O_QA = 0
O_KA = O_QA + GQA_HEADS * HEAD_DIM
O_VA = O_KA + GQA_KV_HEADS * HEAD_DIM
O_QN = O_VA + GQA_KV_HEADS * HEAD_DIM
O_KN = O_QN + NA_HEADS * HEAD_DIM
O_VN = O_KN + NA_HEADS * HEAD_DIM
O_QM = O_VN + NA_HEADS * HEAD_DIM
O_CKV = O_QM + MLA_HEADS * MLA_QK
O_KR = O_CKV + MLA_KV_RANK
O_GATE = O_KR + MLA_ROPE

TM = 256
TQ = 1024
ATTN_LOOKAHEAD = 2
ATTN_KC = 1
KPAD = 128
ONES_ROWS = 16
NA_QROWS = 4
MM_TILES = 2
NA_UNROLL = 10
NA_KBLK = 3
NEG = -1e30
VMEM_LIMIT = 56 * 1024 * 1024

QSCALE_A = HEAD_DIM ** -0.5 * LOG2E
QSCALE_M = MLA_QK ** -0.5 * LOG2E


def _cparams(n_axes):
    return pltpu.CompilerParams(dimension_semantics=("parallel",) * n_axes,
                                vmem_limit_bytes=VMEM_LIMIT)


def _layer_spec(stacked, layer):
    idx = (layer,) + (0,) * (stacked.ndim - 1)
    return pl.BlockSpec((None,) + stacked.shape[1:], lambda *_: idx, pipeline_mode=pl.Buffered(1))


def _const_spec(shape):
    nd = len(shape)
    return pl.BlockSpec(shape, lambda *_: (0,) * nd, pipeline_mode=pl.Buffered(1))


def _rms_mod(x, g, scale, shift):
    ms = jnp.mean(x * x, axis=-1, keepdims=True)
    return (x * lax.rsqrt(ms + EPS) * g) * (1.0 + scale) + shift


def _rms_fm(x, g):
    ms = jnp.mean(x * x, axis=0, keepdims=True)
    return x * lax.rsqrt(ms + EPS) * g


def _rope_fm(x, cos, sin, blk):
    sw = jnp.concatenate([x[blk:2 * blk], x[0:blk], x[3 * blk:4 * blk], x[2 * blk:3 * blk]], axis=0)
    return x * cos + sw * sin


def _mod_kernel(c_ref, w_ref, b_ref, o_ref):
    c = c_ref[...]
    c = (c * jax.nn.sigmoid(c)).astype(BF16)
    w = w_ref[0].astype(BF16)
    o_ref[0] = jnp.dot(c, w, preferred_element_type=F32) + b_ref[0]


def _modulation(cond, w_mod, b_mod):
    depth, d, n = w_mod.shape
    rows = cond.shape[0]
    tn = 1536
    return pl.pallas_call(
        _mod_kernel,
        grid=(depth, n // tn),
        in_specs=[pl.BlockSpec((rows, d), lambda l, j: (0, 0)),
                  pl.BlockSpec((1, d, tn), lambda l, j: (l, 0, j)),
                  pl.BlockSpec((1, 1, tn), lambda l, j: (l, 0, j))],
        out_specs=pl.BlockSpec((1, rows, tn), lambda l, j: (l, 0, j)),
        out_shape=jax.ShapeDtypeStruct((depth, rows, n), F32),
        compiler_params=_cparams(2),
        name="modulation",
    )(cond, w_mod, b_mod.reshape(depth, 1, n))


def _proj_kernel(x_ref, c_ref, mod_ref, g_ref, w_ref, gq_ref, gk_ref, gkv_ref, wuk_ref, wuv_ref,
                 cosa_ref, sina_ref, cosm_ref, sinm_ref,
                 qa_o, ka_o, va_o, qn_o, kn_o, vn_o, qm_o, km_o, vm_o, *, n_lat):
    d = D_MODEL
    x = jnp.where(pl.program_id(0) < n_lat, x_ref[...], c_ref[...])
    m = mod_ref[0]
    h = _rms_mod(x, g_ref[...], m[:, d:2 * d], m[:, 0:d]).astype(BF16)
    tm = x.shape[0]

    def seg(a, b):
        return lax.dot_general(w_ref[a:b, :], h, (((1,), (1,)), ((), ())),
                               preferred_element_type=F32)

    def rope_a(t):
        return _rope_fm(t, cosa_ref[...], sina_ref[...], HEAD_DIM // 4)

    def rope_m(t):
        return _rope_fm(t, cosm_ref[...], sinm_ref[...], MLA_ROPE // 4)

    zeros64 = jnp.zeros((HEAD_DIM, tm), BF16)

    p = seg(O_QA, O_KA)
    group = GQA_HEADS // GQA_KV_HEADS
    for hh in range(GQA_HEADS):
        q = rope_a(_rms_fm(p[hh * HEAD_DIM:(hh + 1) * HEAD_DIM], gq_ref[...])) * QSCALE_A
        kvh = hh // group
        for j in range(KPAD // HEAD_DIM):
            r0 = hh * KPAD + j * HEAD_DIM
            qa_o[0, r0:r0 + HEAD_DIM, :] = q.astype(BF16) if j == kvh else zeros64

    p = seg(O_KA, O_VA)
    ks = [rope_a(_rms_fm(p[g * HEAD_DIM:(g + 1) * HEAD_DIM], gk_ref[...])) for g in range(GQA_KV_HEADS)]
    ka_o[...] = jnp.concatenate(ks, axis=0).T.astype(BF16)
    va_o[0] = seg(O_VA, O_QN).astype(BF16)

    p = seg(O_QN, O_KN)
    for hh in range(NA_HEADS):
        q = (p[hh * HEAD_DIM:(hh + 1) * HEAD_DIM] * QSCALE_A).astype(BF16)
        for j in range(KPAD // HEAD_DIM):
            r0 = hh * KPAD + j * HEAD_DIM
            qn_o[0, r0:r0 + HEAD_DIM, :] = q if j == hh % 2 else zeros64
    kn_o[...] = seg(O_KN, O_VN).T.astype(BF16)
    vn_o[0] = seg(O_VN, O_QM).astype(BF16)

    p = seg(O_QM, O_CKV)
    zeros_pad = jnp.zeros((KPAD - MLA_QK, tm), BF16)
    for hh in range(MLA_HEADS):
        base = hh * MLA_QK
        r0 = hh * KPAD
        qm_o[0, r0:r0 + MLA_NOPE, :] = (p[base:base + MLA_NOPE] * QSCALE_M).astype(BF16)
        qm_o[0, r0 + MLA_NOPE:r0 + MLA_QK, :] = (rope_m(p[base + MLA_NOPE:base + MLA_QK]) * QSCALE_M).astype(BF16)
        qm_o[0, r0 + MLA_QK:r0 + KPAD, :] = zeros_pad

    ckv = _rms_fm(seg(O_CKV, O_KR), gkv_ref[...]).astype(BF16)
    k_nope = jnp.dot(wuk_ref[...], ckv, preferred_element_type=F32)
    vm_o[0] = jnp.dot(wuv_ref[...], ckv, preferred_element_type=F32).astype(BF16)
    k_rope = rope_m(seg(O_KR, O_GATE))
    zpad = jnp.zeros((KPAD - MLA_QK, tm), F32)
    blocks = []
    for hh in range(MLA_HEADS):
        blocks += [k_nope[hh * MLA_NOPE:(hh + 1) * MLA_NOPE], k_rope, zpad]
    km_o[...] = jnp.concatenate(blocks, axis=0).T.astype(BF16)


def _project(xt, ct, mod, layer, batch, g1, w_t, gq, gk, gkv, wuk_t, wuv_t, rope):
    d = xt.shape[1]
    n_lat = xt.shape[0] // TM
    n_ctx = ct.shape[0] // TM
    assert n_ctx == batch
    tps = n_lat // batch
    nt = n_lat + n_ctx
    t_all = nt * TM

    def is_lat(i):
        return i < n_lat

    def kv_blk(i):
        return jnp.where(is_lat(i), (i // tps) * (tps + 1) + i % tps, (i - n_lat) * (tps + 1) + tps)

    def rope_spec(tab):
        return pl.BlockSpec((tab.shape[0], TM), lambda i: (0, jnp.where(is_lat(i), i % tps, tps)))

    in_specs = [
        pl.BlockSpec((TM, d), lambda i: (jnp.minimum(i, n_lat - 1), 0)),
        pl.BlockSpec((TM, d), lambda i: (jnp.maximum(i - n_lat, 0), 0)),
        pl.BlockSpec((None, 1, 1, 6 * d), lambda i: (layer, jnp.where(is_lat(i), i // tps, batch), 0, 0)),
    ] + [_layer_spec(a, layer) for a in (g1, w_t, gq, gk, gkv, wuk_t, wuv_t)] + [rope_spec(tab) for tab in rope]
    nh = NA_HEADS * HEAD_DIM
    ga = GQA_KV_HEADS * HEAD_DIM
    out_shape = [
        jax.ShapeDtypeStruct((nt, GQA_HEADS * KPAD, TM), BF16),
        jax.ShapeDtypeStruct((t_all, KPAD), BF16),
        jax.ShapeDtypeStruct((nt, ga, TM), BF16),
        jax.ShapeDtypeStruct((nt, NA_HEADS * KPAD, TM), BF16),
        jax.ShapeDtypeStruct((t_all, nh), BF16),
        jax.ShapeDtypeStruct((nt, nh, TM), BF16),
        jax.ShapeDtypeStruct((nt, MLA_HEADS * KPAD, TM), BF16),
        jax.ShapeDtypeStruct((t_all, MLA_HEADS * KPAD), BF16),
        jax.ShapeDtypeStruct((nt, MLA_HEADS * HEAD_DIM, TM), BF16),
    ]
    out_specs = [
        pl.BlockSpec((1, GQA_HEADS * KPAD, TM), lambda i: (i, 0, 0)),
        pl.BlockSpec((TM, KPAD), lambda i: (kv_blk(i), 0)),
        pl.BlockSpec((1, ga, TM), lambda i: (kv_blk(i), 0, 0)),
        pl.BlockSpec((1, NA_HEADS * KPAD, TM), lambda i: (i, 0, 0)),
        pl.BlockSpec((TM, nh), lambda i: (kv_blk(i), 0)),
        pl.BlockSpec((1, nh, TM), lambda i: (kv_blk(i), 0, 0)),
        pl.BlockSpec((1, MLA_HEADS * KPAD, TM), lambda i: (i, 0, 0)),
        pl.BlockSpec((TM, MLA_HEADS * KPAD), lambda i: (kv_blk(i), 0)),
        pl.BlockSpec((1, MLA_HEADS * HEAD_DIM, TM), lambda i: (kv_blk(i), 0, 0)),
    ]
    return pl.pallas_call(
        functools.partial(_proj_kernel, n_lat=n_lat),
        grid=(nt,),
        in_specs=in_specs,
        out_specs=out_specs,
        out_shape=out_shape,
        compiler_params=_cparams(1),
        name="project",
    )(xt, ct, mod, g1, w_t, gq, gk, gkv, wuk_t, wuv_t, *rope)


def _attn_kernel(q_ref, k_ref, v_ref, o_ref, s_buf, *, n_chunks, nq, tiles):
    tq = tiles * TM
    dv = o_ref.shape[1]
    ones = jnp.ones((ONES_ROWS, TM), BF16)
    neg = jnp.full((1, tq), NEG, F32)
    steps = [(c, min(ATTN_KC, n_chunks - c)) for c in range(0, n_chunks, ATTN_KC)]
    look = min(ATTN_LOOKAHEAD, len(steps))

    def slot_of(t):
        return t if t < look else look + (t - look) % (look + 1)

    def load_q(i):
        return jnp.concatenate([q_ref[i * tiles + j] for j in range(tiles)], axis=1)

    def produce(q, t, m_prev):
        c, n = steps[t]
        s = jnp.dot(k_ref[c * TM:(c + n) * TM, :], q, preferred_element_type=F32)
        s_buf[slot_of(t), 0:n * TM, :] = s
        m_new = jnp.maximum(m_prev, jnp.max(s, axis=0, keepdims=True))
        return m_new, jnp.exp2(m_prev - m_new)

    def consume(t, m, alpha, acc):
        c, n = steps[t]
        p = jnp.exp2(s_buf[slot_of(t), 0:n * TM, :] - m).astype(BF16)
        v_ext = jnp.concatenate([jnp.concatenate([v_ref[c + j], ones], axis=0) for j in range(n)], axis=1)
        pv = jnp.dot(v_ext, p, preferred_element_type=F32)
        return pv if acc is None else alpha * acc + pv

    def lead_in(q):
        stats, m = [], neg
        for t in range(look):
            m, alpha = produce(q, t, m)
            stats += [m, alpha]
        return tuple(stats)

    def block(i, carry):
        pending = [carry[2 * t:2 * t + 2] for t in range(look)]
        q = load_q(i)
        q_next = load_q(jnp.minimum(i + 1, nq - 1)) if nq > 1 else None
        m_run, m_next_run = pending[-1][0], neg
        nxt = []
        acc = None
        for t in range(len(steps)):
            ahead = t + look
            if ahead < len(steps):
                m_run, alpha = produce(q, ahead, m_run)
                pending.append((m_run, alpha))
            elif nq > 1:
                m_next_run, alpha = produce(q_next, ahead - len(steps), m_next_run)
                nxt += [m_next_run, alpha]
            m, alpha = pending.pop(0)
            acc = consume(t, m, alpha, acc)
        out = (acc[0:dv] / acc[dv:dv + 1]).astype(o_ref.dtype)
        for j in range(tiles):
            o_ref[i * tiles + j] = out[:, j * TM:(j + 1) * TM]
        return tuple(nxt) if nq > 1 else carry

    carry = lead_in(load_q(0))
    if nq > 1:
        lax.fori_loop(0, nq, block, carry)
    else:
        block(0, carry)


def _attention(q, k, v, batch, tps, kcol, vrow, ctx_queries, tq=TM):
    heads = q.shape[1] // KPAD
    per_b = tps + 1
    if ctx_queries:
        q_tiles, tq, n_chunks = 1, TM, 1
        q_spec = pl.BlockSpec((1, KPAD, TM), lambda b, h: (batch * tps + b, h, 0))
        k_spec = pl.BlockSpec((TM, KPAD), lambda b, h: (b * per_b + tps, kcol(h)))
        v_spec = pl.BlockSpec((1, HEAD_DIM, TM), lambda b, h: (b * per_b + tps, vrow(h), 0))
    else:
        q_tiles, n_chunks = tps, per_b
        q_spec = pl.BlockSpec((tps, KPAD, TM), lambda b, h: (b, h, 0))
        k_spec = pl.BlockSpec((per_b * TM, KPAD), lambda b, h: (b, kcol(h)))
        v_spec = pl.BlockSpec((per_b, HEAD_DIM, TM), lambda b, h: (b, vrow(h), 0))
    tiles = tq // TM
    return pl.pallas_call(
        functools.partial(_attn_kernel, n_chunks=n_chunks, nq=q_tiles // tiles, tiles=tiles),
        grid=(batch, heads),
        in_specs=[q_spec, k_spec, v_spec],
        out_specs=pl.BlockSpec((q_tiles, HEAD_DIM, TM), lambda b, h: (b, h, 0)),
        out_shape=jax.ShapeDtypeStruct((batch * q_tiles, heads * HEAD_DIM, TM), BF16),
        scratch_shapes=[pltpu.VMEM((2 * min(ATTN_LOOKAHEAD, n_chunks) + 1, min(ATTN_KC, n_chunks) * TM, tq), F32)],
        compiler_params=_cparams(2),
        name="ctx_attention" if ctx_queries else "attention",
    )(q, k, v)


def _na_kernel(q_ref, k_ref, v_ref, toe_ref, o_ref, s_buf, bias_buf, *, nq, dr_idx):
    win = NA_KBLK * TM
    ones = jnp.ones((ONES_ROWS, TM), BF16)
    static = lambda i: isinstance(i, int)

    lane_blk = lax.broadcasted_iota(jnp.int32, (GRID_W, TM), 1) // GRID_W
    masked = jnp.full((GRID_W, TM), NEG, F32)
    strip = lambda d: masked if d < 0 else toe_ref[0, d]
    for case in range(3):
        for a in range(NA_KBLK * NA_QROWS):
            ids = dr_idx[case][a]
            row = strip(ids[0])
            for b in range(1, NA_QROWS):
                if any(ids[j] != ids[0] for j in range(1, b + 1)):
                    row = jnp.where(lane_blk == b, strip(ids[b]), row)
            bias_buf[case, a * GRID_W:(a + 1) * GRID_W, :] = row

    def kblk(i):
        return min(max(i - 1, 0), nq - NA_KBLK) if static(i) else jnp.clip(i - 1, 0, nq - NA_KBLK)

    def produce(i, slot):
        if static(i):
            case, off = (0 if i == 0 else 2 if i == nq - 1 else 1), kblk(i) * TM
        else:
            case = jnp.where(i == 0, 0, jnp.where(i == nq - 1, 2, 1))
            off = pl.multiple_of(kblk(i) * TM, TM)
        q = q_ref[i]
        s_win = jnp.dot(k_ref[pl.ds(off, win), :], q, preferred_element_type=F32) + bias_buf[case]
        s_ctx = jnp.dot(k_ref[pl.ds(nq * TM, TM), :], q, preferred_element_type=F32)
        s_buf[slot, 0:win, :] = s_win
        s_buf[slot, win:win + TM, :] = s_ctx
        return jnp.maximum(jnp.max(s_win, axis=0, keepdims=True), jnp.max(s_ctx, axis=0, keepdims=True))

    def consume(i, slot, m):
        p = jnp.exp2(s_buf[slot] - m).astype(BF16)
        kb = kblk(i)
        chunks = [v_ref[kb + j] for j in range(NA_KBLK)] + [v_ref[nq]]
        v_all = jnp.concatenate([jnp.concatenate([c, ones], axis=0) for c in chunks], axis=1)
        acc = jnp.dot(v_all, p, preferred_element_type=F32)
        dv = o_ref.shape[1]
        o_ref[i] = (acc[0:dv] / acc[dv:dv + 1]).astype(o_ref.dtype)

    unroll = max(u for u in range(2, NA_UNROLL + 1, 2) if (nq - 2) % u == 0)
    m = produce(0, 0)

    def body(it, m):
        c = unroll * it
        for j in range(unroll):
            m_next = produce(c + j + 1, (j + 1) % 2)
            consume(c + j, j % 2, m)
            m = m_next
        return m

    m = lax.fori_loop(0, (nq - 2) // unroll, body, m)
    m_last = produce(nq - 1, (nq - 1) % 2)
    consume(nq - 2, (nq - 2) % 2, m)
    consume(nq - 1, (nq - 1) % 2, m_last)


def _na_toeplitz(rpb):
    kc = np.arange(GRID_W)[:, None]
    c = np.arange(GRID_W)[None, :]
    cs = np.clip(c - NA_KW // 2, 0, GRID_W - NA_KW)
    valid_c = np.tile((kc >= cs) & (kc < cs + NA_KW), (1, NA_QROWS))
    ci = np.tile(kc - c + (NA_KW - 1), (1, NA_QROWS))
    rpb = rpb.astype(F32) * LOG2E
    toe = jnp.full(rpb.shape[:-1] + (GRID_W, TM), NEG, F32)
    for j in range(2 * NA_KW - 1):
        toe = jnp.where(jnp.asarray(valid_c & (ci == j)), rpb[..., j][..., None, None], toe)
    return toe


def _na_row_index(rows):
    n_kr = NA_KBLK * NA_QROWS
    table = []
    for r0 in (0, NA_QROWS, rows - NA_QROWS):
        ks = min(max(r0 - NA_KH // 2, 0), rows - n_kr)
        per_case = []
        for a in range(n_kr):
            ids = []
            for b in range(NA_QROWS):
                kr, r = ks + a, r0 + b
                rs = min(max(r - NA_KH // 2, 0), rows - NA_KH)
                ids.append(kr - r + NA_KH - 1 if rs <= kr < rs + NA_KH else -1)
            per_case.append(tuple(ids))
        table.append(tuple(per_case))
    return tuple(table)


def _na_attention(q, k, v, toe, layer, batch, tps):
    heads = q.shape[1] // KPAD
    nq = tps
    per_b = tps + 1
    assert nq >= NA_KBLK and nq % 2 == 0
    rows = nq * NA_QROWS
    return pl.pallas_call(
        functools.partial(_na_kernel, nq=nq, dr_idx=_na_row_index(rows)),
        grid=(batch, heads),
        in_specs=[pl.BlockSpec((nq, KPAD, TM), lambda b, h: (b, h, 0)),
                  pl.BlockSpec((per_b * TM, KPAD), lambda b, h: (b, h // 2)),
                  pl.BlockSpec((per_b, HEAD_DIM, TM), lambda b, h: (b, h, 0)),
                  pl.BlockSpec((None, 1) + toe.shape[2:], lambda b, h: (layer, h, 0, 0, 0))],
        out_specs=pl.BlockSpec((nq, HEAD_DIM, TM), lambda b, h: (b, h, 0)),
        out_shape=jax.ShapeDtypeStruct((batch * nq, heads * HEAD_DIM, TM), BF16),
        scratch_shapes=[pltpu.VMEM((2, (NA_KBLK + 1) * TM, TM), F32),
                        pltpu.VMEM((3, NA_KBLK * TM, TM), F32)],
        compiler_params=_cparams(2),
        name="neighbourhood_attention",
    )(q, k, v, toe)


def _merge_mlp_kernel(*refs, final):
    (x_ref, mod_ref, g1_ref, g2_ref, ya_ref, yb_ref, yc_ref, wg_ref, woa_ref, wob_ref, woc_ref,
     wout_ref, w1_ref, w2_ref) = refs[:14]
    o_ref = refs[-1]
    d = D_MODEL
    x = x_ref[...]
    m = mod_ref[0]
    h = _rms_mod(x, g1_ref[...], m[:, d:2 * d], m[:, 0:d]).astype(BF16)
    y = None
    for i, (y_ref, wo_ref) in enumerate(((ya_ref, woa_ref), (yb_ref, wob_ref), (yc_ref, woc_ref))):
        gate = jax.nn.sigmoid(jnp.dot(h, wg_ref[:, i * d:(i + 1) * d], preferred_element_type=F32))
        yt = jnp.concatenate([y_ref[j].astype(F32).T.astype(BF16) for j in range(y_ref.shape[0])],
                             axis=0)
        u = gate * jnp.dot(yt, wo_ref[...], preferred_element_type=F32)
        y = u if y is None else y + u
    a = jnp.dot(y.astype(BF16), wout_ref[...], preferred_element_type=F32)
    x = x + m[:, 2 * d:3 * d] * a

    h = _rms_mod(x, g2_ref[...], m[:, 4 * d:5 * d], m[:, 3 * d:4 * d]).astype(BF16)
    a = None
    for c in range(D_FF // d):
        u = jnp.dot(h, w1_ref[:, c * d:(c + 1) * d], preferred_element_type=F32)
        u = jnp.square(jnp.maximum(u, 0.0)).astype(BF16)
        t = jnp.dot(u, w2_ref[c * d:(c + 1) * d, :], preferred_element_type=F32)
        a = t if a is None else a + t
    out = x + m[:, 5 * d:6 * d] * a
    if final:
        gf = refs[14][...]
        ms = jnp.mean(out * out, axis=-1, keepdims=True)
        out = out * lax.rsqrt(ms + EPS) * gf
    o_ref[...] = out


def _merge_mlp(tok, mod, layer, mod_row, g1, g2, ya, yb, yc, weights, gf):
    t, d = tok.shape
    final = gf is not None
    ts = MM_TILES * TM
    y_spec = pl.BlockSpec((MM_TILES, ya.shape[1], TM), lambda i: (i, 0, 0))
    in_specs = [pl.BlockSpec((ts, d), lambda i: (i, 0)),
                pl.BlockSpec((None, 1, 1, 6 * d), lambda i: (layer, mod_row(i), 0, 0)),
                _layer_spec(g1, layer), _layer_spec(g2, layer),
                y_spec, y_spec, y_spec] + [_layer_spec(w, layer) for w in weights]
    args = [tok, mod, g1, g2, ya, yb, yc, *weights]
    if final:
        in_specs.append(_const_spec((1, d)))
        args.append(gf)
    return pl.pallas_call(
        functools.partial(_merge_mlp_kernel, final=final),
        grid=(t // ts,),
        in_specs=in_specs,
        out_specs=pl.BlockSpec((ts, d), lambda i: (i, 0)),
        out_shape=jax.ShapeDtypeStruct((t, d), F32),
        compiler_params=_cparams(1),
        name="merge_mlp_final" if final else "merge_mlp",
    )(*args)


def _rope_tables(n_tokens, rot_dim):
    t = jnp.arange(n_tokens, dtype=jnp.int32)
    row = (t // GRID_W).astype(F32)
    col = (t % GRID_W).astype(F32)
    half = rot_dim // 2
    inv_freq = ROPE_THETA ** (-jnp.arange(0, half, 2, dtype=F32) / half)
    ar, ac = row[:, None] * inv_freq, col[:, None] * inv_freq
    cr, sr, cc, sn = jnp.cos(ar), jnp.sin(ar), jnp.cos(ac), jnp.sin(ac)
    cos = jnp.concatenate([cr, cr, cc, cc], axis=-1).T
    sin = jnp.concatenate([-sr, sr, -sn, sn], axis=-1).T
    cos = jnp.concatenate([cos, jnp.ones((rot_dim, TM), F32)], axis=1)
    sin = jnp.concatenate([sin, jnp.zeros((rot_dim, TM), F32)], axis=1)
    return cos, sin


def kernel(x, c, ctx, c_ctx, w_mod, b_mod, norm1_g, norm2_g, w_in, gqa_q_norm, gqa_k_norm, na_rpb,
           mla_kv_norm, mla_w_uk, mla_w_uv, w_o_gqa, w_o_na, w_o_mla, w_out, w_mlp1, w_mlp2,
           final_norm_g):
    batch, seq, d = x.shape
    n_ctx = ctx.shape[1]
    depth = w_mod.shape[0]
    assert d == D_MODEL and seq % (NA_QROWS * GRID_W) == 0 and n_ctx == TM
    rows = seq // GRID_W
    assert rows >= NA_KBLK * NA_QROWS
    tps = seq // TM
    tq = min(TQ, seq)

    cond = jnp.concatenate([c, c_ctx[None, :]], axis=0)
    cond = jnp.pad(cond, ((0, -(batch + 1) % 8), (0, 0)))
    mod_all = _modulation(cond, w_mod, b_mod)

    rope = _rope_tables(seq, HEAD_DIM) + _rope_tables(seq, MLA_ROPE)
    toe_all = _na_toeplitz(na_rpb)

    xt = x.reshape(batch * seq, d)
    ct = ctx.reshape(batch * n_ctx, d)
    group = GQA_HEADS // GQA_KV_HEADS

    def lat_row(i):
        return i // (tps // MM_TILES)

    def ctx_row(i):
        return batch

    mod = mod_all.reshape(depth, -1, 1, 6 * d)
    g1 = norm1_g.reshape(depth, 1, d)
    g2 = norm2_g.reshape(depth, 1, d)
    proj_params = (g1, jnp.swapaxes(w_in[:, :, :O_GATE], 1, 2).astype(BF16),
                   gqa_q_norm.reshape(depth, -1, 1), gqa_k_norm.reshape(depth, -1, 1),
                   mla_kv_norm.reshape(depth, -1, 1),
                   jnp.swapaxes(mla_w_uk, 1, 2).astype(BF16), jnp.swapaxes(mla_w_uv, 1, 2).astype(BF16))
    weights = (w_in[:, :, O_GATE:].astype(BF16), w_o_gqa.astype(BF16), w_o_na.astype(BF16),
               w_o_mla.astype(BF16), w_out.astype(BF16), w_mlp1.astype(BF16), w_mlp2.astype(BF16))

    for l in range(depth):
        with_ctx = l < depth - 1
        qa, ka, va, qn, kn, vn, qm, km, vm = _project(xt, ct, mod, l, batch, *proj_params, rope)

        att_a = functools.partial(_attention, qa, ka, va, batch, tps, lambda h: 0, lambda h: h // group)
        att_n = functools.partial(_attention, qn, kn, vn, batch, tps, lambda h: h // 2, lambda h: h)
        att_m = functools.partial(_attention, qm, km, vm, batch, tps, lambda h: h, lambda h: h)
        ya = att_a(False, tq)
        yb = _na_attention(qn, kn, vn, toe_all, l, batch, tps)
        yc = att_m(False, tq)

        gf = None if with_ctx else final_norm_g.reshape(1, d)
        xt = _merge_mlp(xt, mod, l, lat_row, g1, g2, ya, yb, yc, weights, gf)
        if with_ctx:
            ct = _merge_mlp(ct, mod, l, ctx_row, g1, g2, att_a(True), att_n(True), att_m(True), weights, None)

    return xt.reshape(batch, seq, d)
```

```python
import functools
import math

import numpy as np
import jax
import jax.numpy as jnp
from jax import lax
from jax.experimental import pallas as pl
from jax.experimental.pallas import tpu as pltpu

F32 = jnp.float32
BF16 = jnp.bfloat16

D_MODEL = 1024
GRID_W = 64
HEAD_DIM = 64
GQA_HEADS = 8
GQA_KV_HEADS = 2
NA_HEADS = 8
NA_KH = 8
NA_KW = 16
MLA_HEADS = 8
MLA_NOPE = 64
MLA_ROPE = 32
MLA_QK = MLA_NOPE + MLA_ROPE
MLA_KV_RANK = 256
D_FF = 4 * D_MODEL
ROPE_THETA = 10000.0
EPS = 1e-6
LOG2E = math.log2(math.e)

O_QA = 0
O_KA = O_QA + GQA_HEADS * HEAD_DIM
O_VA = O_KA + GQA_KV_HEADS * HEAD_DIM
O_QN = O_VA + GQA_KV_HEADS * HEAD_DIM
O_KN = O_QN + NA_HEADS * HEAD_DIM
O_VN = O_KN + NA_HEADS * HEAD_DIM
O_QM = O_VN + NA_HEADS * HEAD_DIM
O_CKV = O_QM + MLA_HEADS * MLA_QK
O_KR = O_CKV + MLA_KV_RANK
O_GATE = O_KR + MLA_ROPE

TM = 256
TQ = 1024
ATTN_LOOKAHEAD = 3
ATTN_KC = 1
KPAD = 128
ONES_ROWS = 16
NA_QROWS = 4
MM_TILES = 2
NA_UNROLL = 10
NA_KBLK = 3
NEG = -1e30
VMEM_LIMIT = 56 * 1024 * 1024

QSCALE_A = HEAD_DIM ** -0.5 * LOG2E
QSCALE_M = MLA_QK ** -0.5 * LOG2E


def _cparams(n_axes):
    return pltpu.CompilerParams(dimension_semantics=("parallel",) * n_axes,
                                vmem_limit_bytes=VMEM_LIMIT)


def _layer_spec(stacked, layer):
    idx = (layer,) + (0,) * (stacked.ndim - 1)
    return pl.BlockSpec((None,) + stacked.shape[1:], lambda *_: idx, pipeline_mode=pl.Buffered(1))


def _const_spec(shape):
    nd = len(shape)
    return pl.BlockSpec(shape, lambda *_: (0,) * nd, pipeline_mode=pl.Buffered(1))


def _rms_mod(x, g, scale, shift):
    ms = jnp.mean(x * x, axis=-1, keepdims=True)
    return (x * lax.rsqrt(ms + EPS) * g) * (1.0 + scale) + shift


def _rms_fm(x, g):
    ms = jnp.mean(x * x, axis=0, keepdims=True)
    return x * lax.rsqrt(ms + EPS) * g


def _rope_fm(x, cos, sin, blk):
    sw = jnp.concatenate([x[blk:2 * blk], x[0:blk], x[3 * blk:4 * blk], x[2 * blk:3 * blk]], axis=0)
    return x * cos + sw * sin


def _mod_kernel(c_ref, w_ref, b_ref, o_ref):
    c = c_ref[...]
    c = (c * jax.nn.sigmoid(c)).astype(BF16)
    w = w_ref[0].astype(BF16)
    o_ref[0] = jnp.dot(c, w, preferred_element_type=F32) + b_ref[0]


def _modulation(cond, w_mod, b_mod):
    depth, d, n = w_mod.shape
    rows = cond.shape[0]
    tn = 1536
    return pl.pallas_call(
        _mod_kernel,
        grid=(depth, n // tn),
        in_specs=[pl.BlockSpec((rows, d), lambda l, j: (0, 0)),
                  pl.BlockSpec((1, d, tn), lambda l, j: (l, 0, j)),
                  pl.BlockSpec((1, 1, tn), lambda l, j: (l, 0, j))],
        out_specs=pl.BlockSpec((1, rows, tn), lambda l, j: (l, 0, j)),
        out_shape=jax.ShapeDtypeStruct((depth, rows, n), F32),
        compiler_params=_cparams(2),
        name="modulation",
    )(cond, w_mod, b_mod.reshape(depth, 1, n))


def _proj_kernel(x_ref, c_ref, mod_ref, g_ref, w_ref, gq_ref, gk_ref, gkv_ref, wuk_ref, wuv_ref,
                 cosa_ref, sina_ref, cosm_ref, sinm_ref,
                 qa_o, ka_o, va_o, qn_o, kn_o, vn_o, qm_o, km_o, vm_o, *, n_lat):
    d = D_MODEL
    x = jnp.where(pl.program_id(0) < n_lat, x_ref[...], c_ref[...])
    m = mod_ref[0]
    h = _rms_mod(x, g_ref[...], m[:, d:2 * d], m[:, 0:d]).astype(BF16)
    tm = x.shape[0]

    def seg(a, b):
        return lax.dot_general(w_ref[a:b, :], h, (((1,), (1,)), ((), ())),
                               preferred_element_type=F32)

    def rope_a(t):
        return _rope_fm(t, cosa_ref[...], sina_ref[...], HEAD_DIM // 4)

    def rope_m(t):
        return _rope_fm(t, cosm_ref[...], sinm_ref[...], MLA_ROPE // 4)

    zeros64 = jnp.zeros((HEAD_DIM, tm), BF16)

    p = seg(O_QA, O_KA)
    group = GQA_HEADS // GQA_KV_HEADS
    for hh in range(GQA_HEADS):
        q = rope_a(_rms_fm(p[hh * HEAD_DIM:(hh + 1) * HEAD_DIM], gq_ref[...])) * QSCALE_A
        kvh = hh // group
        for j in range(KPAD // HEAD_DIM):
            r0 = hh * KPAD + j * HEAD_DIM
            qa_o[0, r0:r0 + HEAD_DIM, :] = q.astype(BF16) if j == kvh else zeros64

    p = seg(O_KA, O_VA)
    ks = [rope_a(_rms_fm(p[g * HEAD_DIM:(g + 1) * HEAD_DIM], gk_ref[...])) for g in range(GQA_KV_HEADS)]
    ka_o[...] = jnp.concatenate(ks, axis=0).T.astype(BF16)
    va_o[0] = seg(O_VA, O_QN).astype(BF16)

    p = seg(O_QN, O_KN)
    for hh in range(NA_HEADS):
        q = (p[hh * HEAD_DIM:(hh + 1) * HEAD_DIM] * QSCALE_A).astype(BF16)
        for j in range(KPAD // HEAD_DIM):
            r0 = hh * KPAD + j * HEAD_DIM
            qn_o[0, r0:r0 + HEAD_DIM, :] = q if j == hh % 2 else zeros64
    kn_o[...] = seg(O_KN, O_VN).T.astype(BF16)
    vn_o[0] = seg(O_VN, O_QM).astype(BF16)

    p = seg(O_QM, O_CKV)
    zeros_pad = jnp.zeros((KPAD - MLA_QK, tm), BF16)
    for hh in range(MLA_HEADS):
        base = hh * MLA_QK
        r0 = hh * KPAD
        qm_o[0, r0:r0 + MLA_NOPE, :] = (p[base:base + MLA_NOPE] * QSCALE_M).astype(BF16)
        qm_o[0, r0 + MLA_NOPE:r0 + MLA_QK, :] = (rope_m(p[base + MLA_NOPE:base + MLA_QK]) * QSCALE_M).astype(BF16)
        qm_o[0, r0 + MLA_QK:r0 + KPAD, :] = zeros_pad

    ckv = _rms_fm(seg(O_CKV, O_KR), gkv_ref[...]).astype(BF16)
    k_nope = jnp.dot(wuk_ref[...], ckv, preferred_element_type=F32)
    vm_o[0] = jnp.dot(wuv_ref[...], ckv, preferred_element_type=F32).astype(BF16)
    k_rope = rope_m(seg(O_KR, O_GATE))
    zpad = jnp.zeros((KPAD - MLA_QK, tm), F32)
    blocks = []
    for hh in range(MLA_HEADS):
        blocks += [k_nope[hh * MLA_NOPE:(hh + 1) * MLA_NOPE], k_rope, zpad]
    km_o[...] = jnp.concatenate(blocks, axis=0).T.astype(BF16)


def _project(xt, ct, mod, layer, batch, g1, w_t, gq, gk, gkv, wuk_t, wuv_t, rope):
    d = xt.shape[1]
    n_lat = xt.shape[0] // TM
    n_ctx = ct.shape[0] // TM
    assert n_ctx == batch
    tps = n_lat // batch
    nt = n_lat + n_ctx
    t_all = nt * TM

    def is_lat(i):
        return i < n_lat

    def kv_blk(i):
        return jnp.where(is_lat(i), (i // tps) * (tps + 1) + i % tps, (i - n_lat) * (tps + 1) + tps)

    def rope_spec(tab):
        return pl.BlockSpec((tab.shape[0], TM), lambda i: (0, jnp.where(is_lat(i), i % tps, tps)))

    in_specs = [
        pl.BlockSpec((TM, d), lambda i: (jnp.minimum(i, n_lat - 1), 0)),
        pl.BlockSpec((TM, d), lambda i: (jnp.maximum(i - n_lat, 0), 0)),
        pl.BlockSpec((None, 1, 1, 6 * d), lambda i: (layer, jnp.where(is_lat(i), i // tps, batch), 0, 0)),
    ] + [_layer_spec(a, layer) for a in (g1, w_t, gq, gk, gkv, wuk_t, wuv_t)] + [rope_spec(tab) for tab in rope]
    nh = NA_HEADS * HEAD_DIM
    ga = GQA_KV_HEADS * HEAD_DIM
    out_shape = [
        jax.ShapeDtypeStruct((nt, GQA_HEADS * KPAD, TM), BF16),
        jax.ShapeDtypeStruct((t_all, KPAD), BF16),
        jax.ShapeDtypeStruct((nt, ga, TM), BF16),
        jax.ShapeDtypeStruct((nt, NA_HEADS * KPAD, TM), BF16),
        jax.ShapeDtypeStruct((t_all, nh), BF16),
        jax.ShapeDtypeStruct((nt, nh, TM), BF16),
        jax.ShapeDtypeStruct((nt, MLA_HEADS * KPAD, TM), BF16),
        jax.ShapeDtypeStruct((t_all, MLA_HEADS * KPAD), BF16),
        jax.ShapeDtypeStruct((nt, MLA_HEADS * HEAD_DIM, TM), BF16),
    ]
    out_specs = [
        pl.BlockSpec((1, GQA_HEADS * KPAD, TM), lambda i: (i, 0, 0)),
        pl.BlockSpec((TM, KPAD), lambda i: (kv_blk(i), 0)),
        pl.BlockSpec((1, ga, TM), lambda i: (kv_blk(i), 0, 0)),
        pl.BlockSpec((1, NA_HEADS * KPAD, TM), lambda i: (i, 0, 0)),
        pl.BlockSpec((TM, nh), lambda i: (kv_blk(i), 0)),
        pl.BlockSpec((1, nh, TM), lambda i: (kv_blk(i), 0, 0)),
        pl.BlockSpec((1, MLA_HEADS * KPAD, TM), lambda i: (i, 0, 0)),
        pl.BlockSpec((TM, MLA_HEADS * KPAD), lambda i: (kv_blk(i), 0)),
        pl.BlockSpec((1, MLA_HEADS * HEAD_DIM, TM), lambda i: (kv_blk(i), 0, 0)),
    ]
    return pl.pallas_call(
        functools.partial(_proj_kernel, n_lat=n_lat),
        grid=(nt,),
        in_specs=in_specs,
        out_specs=out_specs,
        out_shape=out_shape,
        compiler_params=_cparams(1),
        name="project",
    )(xt, ct, mod, g1, w_t, gq, gk, gkv, wuk_t, wuv_t, *rope)


def _attn_kernel(q_ref, k_ref, v_ref, o_ref, s_buf, *, n_chunks, nq, tiles):
    tq = tiles * TM
    dv = o_ref.shape[1]
    ones = jnp.ones((ONES_ROWS, TM), BF16)
    neg = jnp.full((1, tq), NEG, F32)
    steps = [(c, min(ATTN_KC, n_chunks - c)) for c in range(0, n_chunks, ATTN_KC)]
    look = min(ATTN_LOOKAHEAD, len(steps))

    def slot_of(t):
        return t if t < look else look + (t - look) % (look + 1)

    def load_q(i):
        return jnp.concatenate([q_ref[i * tiles + j] for j in range(tiles)], axis=1)

    def produce(q, t, m_prev):
        c, n = steps[t]
        s = jnp.dot(k_ref[c * TM:(c + n) * TM, :], q, preferred_element_type=F32)
        s_buf[slot_of(t), 0:n * TM, :] = s
        m_new = jnp.maximum(m_prev, jnp.max(s, axis=0, keepdims=True))
        return m_new, jnp.exp2(m_prev - m_new)

    def consume(t, m, alpha, acc):
        c, n = steps[t]
        p = jnp.exp2(s_buf[slot_of(t), 0:n * TM, :] - m).astype(BF16)
        v_ext = jnp.concatenate([jnp.concatenate([v_ref[c + j], ones], axis=0) for j in range(n)], axis=1)
        pv = jnp.dot(v_ext, p, preferred_element_type=F32)
        return pv if acc is None else alpha * acc + pv

    def lead_in(q):
        stats, m = [], neg
        for t in range(look):
            m, alpha = produce(q, t, m)
            stats += [m, alpha]
        return tuple(stats)

    def block(i, carry):
        pending = [carry[2 * t:2 * t + 2] for t in range(look)]
        q = load_q(i)
        q_next = load_q(jnp.minimum(i + 1, nq - 1)) if nq > 1 else None
        m_run, m_next_run = pending[-1][0], neg
        nxt = []
        acc = None
        for t in range(len(steps)):
            ahead = t + look
            if ahead < len(steps):
                m_run, alpha = produce(q, ahead, m_run)
                pending.append((m_run, alpha))
            elif nq > 1:
                m_next_run, alpha = produce(q_next, ahead - len(steps), m_next_run)
                nxt += [m_next_run, alpha]
            m, alpha = pending.pop(0)
            acc = consume(t, m, alpha, acc)
        out = (acc[0:dv] / acc[dv:dv + 1]).astype(o_ref.dtype)
        for j in range(tiles):
            o_ref[i * tiles + j] = out[:, j * TM:(j + 1) * TM]
        return tuple(nxt) if nq > 1 else carry

    carry = lead_in(load_q(0))
    if nq > 1:
        lax.fori_loop(0, nq, block, carry)
    else:
        block(0, carry)


def _attention(q, k, v, batch, tps, kcol, vrow, ctx_queries, tq=TM):
    heads = q.shape[1] // KPAD
    per_b = tps + 1
    if ctx_queries:
        q_tiles, tq, n_chunks = 1, TM, 1
        q_spec = pl.BlockSpec((1, KPAD, TM), lambda b, h: (batch * tps + b, h, 0))
        k_spec = pl.BlockSpec((TM, KPAD), lambda b, h: (b * per_b + tps, kcol(h)))
        v_spec = pl.BlockSpec((1, HEAD_DIM, TM), lambda b, h: (b * per_b + tps, vrow(h), 0))
    else:
        q_tiles, n_chunks = tps, per_b
        q_spec = pl.BlockSpec((tps, KPAD, TM), lambda b, h: (b, h, 0))
        k_spec = pl.BlockSpec((per_b * TM, KPAD), lambda b, h: (b, kcol(h)))
        v_spec = pl.BlockSpec((per_b, HEAD_DIM, TM), lambda b, h: (b, vrow(h), 0))
    tiles = tq // TM
    return pl.pallas_call(
        functools.partial(_attn_kernel, n_chunks=n_chunks, nq=q_tiles // tiles, tiles=tiles),
        grid=(batch, heads),
        in_specs=[q_spec, k_spec, v_spec],
        out_specs=pl.BlockSpec((q_tiles, HEAD_DIM, TM), lambda b, h: (b, h, 0)),
        out_shape=jax.ShapeDtypeStruct((batch * q_tiles, heads * HEAD_DIM, TM), BF16),
        scratch_shapes=[pltpu.VMEM((2 * min(ATTN_LOOKAHEAD, n_chunks) + 1, min(ATTN_KC, n_chunks) * TM, tq), F32)],
        compiler_params=_cparams(2),
        name="ctx_attention" if ctx_queries else "attention",
    )(q, k, v)


def _na_kernel(q_ref, k_ref, v_ref, toe_ref, o_ref, s_buf, bias_buf, *, nq, dr_idx):
    win = NA_KBLK * TM
    ones = jnp.ones((ONES_ROWS, TM), BF16)
    static = lambda i: isinstance(i, int)

    lane_blk = lax.broadcasted_iota(jnp.int32, (GRID_W, TM), 1) // GRID_W
    masked = jnp.full((GRID_W, TM), NEG, F32)
    strip = lambda d: masked if d < 0 else toe_ref[0, d]
    for case in range(3):
        for a in range(NA_KBLK * NA_QROWS):
            ids = dr_idx[case][a]
            row = strip(ids[0])
            for b in range(1, NA_QROWS):
                if any(ids[j] != ids[0] for j in range(1, b + 1)):
                    row = jnp.where(lane_blk == b, strip(ids[b]), row)
            bias_buf[case, a * GRID_W:(a + 1) * GRID_W, :] = row

    def kblk(i):
        return min(max(i - 1, 0), nq - NA_KBLK) if static(i) else jnp.clip(i - 1, 0, nq - NA_KBLK)

    def produce(i, slot):
        if static(i):
            case, off = (0 if i == 0 else 2 if i == nq - 1 else 1), kblk(i) * TM
        else:
            case = jnp.where(i == 0, 0, jnp.where(i == nq - 1, 2, 1))
            off = pl.multiple_of(kblk(i) * TM, TM)
        q = q_ref[i]
        s_win = jnp.dot(k_ref[pl.ds(off, win), :], q, preferred_element_type=F32) + bias_buf[case]
        s_ctx = jnp.dot(k_ref[pl.ds(nq * TM, TM), :], q, preferred_element_type=F32)
        s_buf[slot, 0:win, :] = s_win
        s_buf[slot, win:win + TM, :] = s_ctx
        return jnp.maximum(jnp.max(s_win, axis=0, keepdims=True), jnp.max(s_ctx, axis=0, keepdims=True))

    def consume(i, slot, m):
        p = jnp.exp2(s_buf[slot] - m).astype(BF16)
        kb = kblk(i)
        chunks = [v_ref[kb + j] for j in range(NA_KBLK)] + [v_ref[nq]]
        v_all = jnp.concatenate([jnp.concatenate([c, ones], axis=0) for c in chunks], axis=1)
        acc = jnp.dot(v_all, p, preferred_element_type=F32)
        dv = o_ref.shape[1]
        o_ref[i] = (acc[0:dv] / acc[dv:dv + 1]).astype(o_ref.dtype)

    unroll = max(u for u in range(2, NA_UNROLL + 1, 2) if (nq - 2) % u == 0)
    m = produce(0, 0)

    def body(it, m):
        c = unroll * it
        for j in range(unroll):
            m_next = produce(c + j + 1, (j + 1) % 2)
            consume(c + j, j % 2, m)
            m = m_next
        return m

    m = lax.fori_loop(0, (nq - 2) // unroll, body, m)
    m_last = produce(nq - 1, (nq - 1) % 2)
    consume(nq - 2, (nq - 2) % 2, m)
    consume(nq - 1, (nq - 1) % 2, m_last)


def _na_toeplitz(rpb):
    kc = np.arange(GRID_W)[:, None]
    c = np.arange(GRID_W)[None, :]
    cs = np.clip(c - NA_KW // 2, 0, GRID_W - NA_KW)
    valid_c = np.tile((kc >= cs) & (kc < cs + NA_KW), (1, NA_QROWS))
    ci = np.tile(kc - c + (NA_KW - 1), (1, NA_QROWS))
    rpb = rpb.astype(F32) * LOG2E
    toe = jnp.full(rpb.shape[:-1] + (GRID_W, TM), NEG, F32)
    for j in range(2 * NA_KW - 1):
        toe = jnp.where(jnp.asarray(valid_c & (ci == j)), rpb[..., j][..., None, None], toe)
    return toe


def _na_row_index(rows):
    n_kr = NA_KBLK * NA_QROWS
    table = []
    for r0 in (0, NA_QROWS, rows - NA_QROWS):
        ks = min(max(r0 - NA_KH // 2, 0), rows - n_kr)
        per_case = []
        for a in range(n_kr):
            ids = []
            for b in range(NA_QROWS):
                kr, r = ks + a, r0 + b
                rs = min(max(r - NA_KH // 2, 0), rows - NA_KH)
                ids.append(kr - r + NA_KH - 1 if rs <= kr < rs + NA_KH else -1)
            per_case.append(tuple(ids))
        table.append(tuple(per_case))
    return tuple(table)


def _na_attention(q, k, v, toe, layer, batch, tps):
    heads = q.shape[1] // KPAD
    nq = tps
    per_b = tps + 1
    assert nq >= NA_KBLK and nq % 2 == 0
    rows = nq * NA_QROWS
    return pl.pallas_call(
        functools.partial(_na_kernel, nq=nq, dr_idx=_na_row_index(rows)),
        grid=(batch, heads),
        in_specs=[pl.BlockSpec((nq, KPAD, TM), lambda b, h: (b, h, 0)),
                  pl.BlockSpec((per_b * TM, KPAD), lambda b, h: (b, h // 2)),
                  pl.BlockSpec((per_b, HEAD_DIM, TM), lambda b, h: (b, h, 0)),
                  pl.BlockSpec((None, 1) + toe.shape[2:], lambda b, h: (layer, h, 0, 0, 0))],
        out_specs=pl.BlockSpec((nq, HEAD_DIM, TM), lambda b, h: (b, h, 0)),
        out_shape=jax.ShapeDtypeStruct((batch * nq, heads * HEAD_DIM, TM), BF16),
        scratch_shapes=[pltpu.VMEM((2, (NA_KBLK + 1) * TM, TM), F32),
                        pltpu.VMEM((3, NA_KBLK * TM, TM), F32)],
        compiler_params=_cparams(2),
        name="neighbourhood_attention",
    )(q, k, v, toe)


def _merge_mlp_kernel(*refs, final):
    (x_ref, mod_ref, g1_ref, g2_ref, ya_ref, yb_ref, yc_ref, wg_ref, woa_ref, wob_ref, woc_ref,
     wout_ref, w1_ref, w2_ref) = refs[:14]
    o_ref = refs[-1]
    d = D_MODEL
    x = x_ref[...]
    m = mod_ref[0]
    h = _rms_mod(x, g1_ref[...], m[:, d:2 * d], m[:, 0:d]).astype(BF16)
    y = None
    for i, (y_ref, wo_ref) in enumerate(((ya_ref, woa_ref), (yb_ref, wob_ref), (yc_ref, woc_ref))):
        gate = jax.nn.sigmoid(jnp.dot(h, wg_ref[:, i * d:(i + 1) * d], preferred_element_type=F32))
        yt = jnp.concatenate([y_ref[j].astype(F32).T.astype(BF16) for j in range(y_ref.shape[0])],
                             axis=0)
        u = gate * jnp.dot(yt, wo_ref[...], preferred_element_type=F32)
        y = u if y is None else y + u
    a = jnp.dot(y.astype(BF16), wout_ref[...], preferred_element_type=F32)
    x = x + m[:, 2 * d:3 * d] * a

    h = _rms_mod(x, g2_ref[...], m[:, 4 * d:5 * d], m[:, 3 * d:4 * d]).astype(BF16)
    a = None
    for c in range(D_FF // d):
        u = jnp.dot(h, w1_ref[:, c * d:(c + 1) * d], preferred_element_type=F32)
        u = jnp.square(jnp.maximum(u, 0.0)).astype(BF16)
        t = jnp.dot(u, w2_ref[c * d:(c + 1) * d, :], preferred_element_type=F32)
        a = t if a is None else a + t
    out = x + m[:, 5 * d:6 * d] * a
    if final:
        gf = refs[14][...]
        ms = jnp.mean(out * out, axis=-1, keepdims=True)
        out = out * lax.rsqrt(ms + EPS) * gf
    o_ref[...] = out


def _merge_mlp(tok, mod, layer, mod_row, g1, g2, ya, yb, yc, weights, gf):
    t, d = tok.shape
    final = gf is not None
    ts = MM_TILES * TM
    y_spec = pl.BlockSpec((MM_TILES, ya.shape[1], TM), lambda i: (i, 0, 0))
    in_specs = [pl.BlockSpec((ts, d), lambda i: (i, 0)),
                pl.BlockSpec((None, 1, 1, 6 * d), lambda i: (layer, mod_row(i), 0, 0)),
                _layer_spec(g1, layer), _layer_spec(g2, layer),
                y_spec, y_spec, y_spec] + [_layer_spec(w, layer) for w in weights]
    args = [tok, mod, g1, g2, ya, yb, yc, *weights]
    if final:
        in_specs.append(_const_spec((1, d)))
        args.append(gf)
    return pl.pallas_call(
        functools.partial(_merge_mlp_kernel, final=final),
        grid=(t // ts,),
        in_specs=in_specs,
        out_specs=pl.BlockSpec((ts, d), lambda i: (i, 0)),
        out_shape=jax.ShapeDtypeStruct((t, d), F32),
        compiler_params=_cparams(1),
        name="merge_mlp_final" if final else "merge_mlp",
    )(*args)


def _rope_tables(n_tokens, rot_dim):
    t = jnp.arange(n_tokens, dtype=jnp.int32)
    row = (t // GRID_W).astype(F32)
    col = (t % GRID_W).astype(F32)
    half = rot_dim // 2
    inv_freq = ROPE_THETA ** (-jnp.arange(0, half, 2, dtype=F32) / half)
    ar, ac = row[:, None] * inv_freq, col[:, None] * inv_freq
    cr, sr, cc, sn = jnp.cos(ar), jnp.sin(ar), jnp.cos(ac), jnp.sin(ac)
    cos = jnp.concatenate([cr, cr, cc, cc], axis=-1).T
    sin = jnp.concatenate([-sr, sr, -sn, sn], axis=-1).T
    cos = jnp.concatenate([cos, jnp.ones((rot_dim, TM), F32)], axis=1)
    sin = jnp.concatenate([sin, jnp.zeros((rot_dim, TM), F32)], axis=1)
    return cos, sin


def kernel(x, c, ctx, c_ctx, w_mod, b_mod, norm1_g, norm2_g, w_in, gqa_q_norm, gqa_k_norm, na_rpb,
           mla_kv_norm, mla_w_uk, mla_w_uv, w_o_gqa, w_o_na, w_o_mla, w_out, w_mlp1, w_mlp2,
           final_norm_g):
    batch, seq, d = x.shape
    n_ctx = ctx.shape[1]
    depth = w_mod.shape[0]
    assert d == D_MODEL and seq % (NA_QROWS * GRID_W) == 0 and n_ctx == TM
    rows = seq // GRID_W
    assert rows >= NA_KBLK * NA_QROWS
    tps = seq // TM
    tq = min(TQ, seq)

    cond = jnp.concatenate([c, c_ctx[None, :]], axis=0)
    cond = jnp.pad(cond, ((0, -(batch + 1) % 8), (0, 0)))
    mod_all = _modulation(cond, w_mod, b_mod)

    rope = _rope_tables(seq, HEAD_DIM) + _rope_tables(seq, MLA_ROPE)
    toe_all = _na_toeplitz(na_rpb)

    xt = x.reshape(batch * seq, d)
    ct = ctx.reshape(batch * n_ctx, d)
    group = GQA_HEADS // GQA_KV_HEADS

    def lat_row(i):
        return i // (tps // MM_TILES)

    def ctx_row(i):
        return batch

    mod = mod_all.reshape(depth, -1, 1, 6 * d)
    g1 = norm1_g.reshape(depth, 1, d)
    g2 = norm2_g.reshape(depth, 1, d)
    proj_params = (g1, jnp.swapaxes(w_in[:, :, :O_GATE], 1, 2).astype(BF16),
                   gqa_q_norm.reshape(depth, -1, 1), gqa_k_norm.reshape(depth, -1, 1),
                   mla_kv_norm.reshape(depth, -1, 1),
                   jnp.swapaxes(mla_w_uk, 1, 2).astype(BF16), jnp.swapaxes(mla_w_uv, 1, 2).astype(BF16))
    weights = (w_in[:, :, O_GATE:].astype(BF16), w_o_gqa.astype(BF16), w_o_na.astype(BF16),
               w_o_mla.astype(BF16), w_out.astype(BF16), w_mlp1.astype(BF16), w_mlp2.astype(BF16))

    for l in range(depth):
        with_ctx = l < depth - 1
        qa, ka, va, qn, kn, vn, qm, km, vm = _project(xt, ct, mod, l, batch, *proj_params, rope)

        att_a = functools.partial(_attention, qa, ka, va, batch, tps, lambda h: 0, lambda h: h // group)
        att_n = functools.partial(_attention, qn, kn, vn, batch, tps, lambda h: h // 2, lambda h: h)
        att_m = functools.partial(_attention, qm, km, vm, batch, tps, lambda h: h, lambda h: h)
        ya = att_a(False, tq)
        yb = _na_attention(qn, kn, vn, toe_all, l, batch, tps)
        yc = att_m(False, tq)

        gf = None if with_ctx else final_norm_g.reshape(1, d)
        xt = _merge_mlp(xt, mod, l, lat_row, g1, g2, ya, yb, yc, weights, gf)
        if with_ctx:
            ct = _merge_mlp(ct, mod, l, ctx_row, g1, g2, att_a(True), att_n(True), att_m(True), weights, None)

    return xt.reshape(batch, seq, d)
```
